```python
import jax, jax.numpy as jnp
from jax import lax
import numpy as np

D_MODEL = 1024
BATCH = 16
SEQ = 2048
DEPTH = 1

N_META = 16
GRID_W = 64
NA_HEADS = 8
NA_HEAD_DIM = 64
NA_WIDTH = NA_HEADS * NA_HEAD_DIM
NA_KH_MAX = 8
NA_KW = 16
GLA_HEADS = 4
GLA_DK = 64
GLA_DV = 128
GLA_KEY_WIDTH = GLA_HEADS * GLA_DK
GLA_VAL_WIDTH = GLA_HEADS * GLA_DV
GLA_GATE_RANK = 16
GLA_GATE_TAU = 16.0
GLA_CHUNK = 64
MIX_WIDTH = NA_WIDTH + GLA_VAL_WIDTH
D_FF = -(-8 * D_MODEL // (3 * 256)) * 256
RMS_EPS = 1e-6
NEG_INF = -1e30
IN_SPLIT_SIZES = (NA_WIDTH, NA_WIDTH, NA_WIDTH,
                  GLA_KEY_WIDTH, GLA_KEY_WIDTH,
                  GLA_VAL_WIDTH, GLA_VAL_WIDTH,
                  GLA_GATE_RANK, GLA_GATE_RANK)
IN_WIDTH = sum(IN_SPLIT_SIZES)

kernel_name = "hybrid_na_gla_bidir_block"


def rms_norm(x, gain, eps=RMS_EPS):
    xf = x.astype(jnp.float32)
    xf = xf * lax.rsqrt(jnp.mean(xf * xf, axis=-1, keepdims=True) + eps)
    return (xf * gain.astype(jnp.float32)).astype(x.dtype)


def neighbourhood_attention(q, k, v, rpb, meta_bias):
    B, L, H, d = q.shape
    N = L - N_META
    rows = N // GRID_W
    kh = min(NA_KH_MAX, rows)
    q = q * (d ** -0.5)
    qm, qg = q[:, :N_META], q[:, N_META:]
    km, kg = k[:, :N_META], k[:, N_META:]
    vm, vg = v[:, :N_META], v[:, N_META:]
    qg = qg.reshape(B, rows, GRID_W, H, d).transpose(1, 0, 3, 2, 4)
    kg = kg.reshape(B, rows, GRID_W, H, d).transpose(0, 3, 1, 2, 4)
    vg = vg.reshape(B, rows, GRID_W, H, d).transpose(0, 3, 1, 2, 4)
    km_h = km.transpose(0, 2, 1, 3)
    vm_h = vm.transpose(0, 2, 1, 3)

    cols = jnp.arange(GRID_W)
    col_start = jnp.clip(cols - NA_KW // 2, 0, GRID_W - NA_KW)
    col_valid = (cols[None, :] >= col_start[:, None]) & (cols[None, :] < col_start[:, None] + NA_KW)
    dc_idx = jnp.clip(cols[None, :] - cols[:, None], -(NA_KW - 1), NA_KW - 1) + (NA_KW - 1)
    row_ids = jnp.arange(rows)
    row_start = jnp.clip(row_ids - kh // 2, 0, rows - kh)

    def row_block(args):
        i, s, q_i = args
        k_win = lax.dynamic_slice_in_dim(kg, s, kh, axis=2)
        v_win = lax.dynamic_slice_in_dim(vg, s, kh, axis=2)
        dr_idx = s + jnp.arange(kh) - i + (NA_KH_MAX - 1)
        bias = rpb[:, dr_idx[:, None, None], dc_idx[None]]
        s_win = jnp.einsum('bhqd,bhrkd->bhqrk', q_i, k_win).astype(jnp.float32)
        s_win = s_win + bias.transpose(0, 2, 1, 3)[None].astype(jnp.float32)
        s_win = jnp.where(col_valid[:, None, :], s_win, NEG_INF)
        s_meta = jnp.einsum('bhqd,bhmd->bhqm', q_i, km_h).astype(jnp.float32)
        s_meta = s_meta + meta_bias[None, :, None, :].astype(jnp.float32)
        scores = jnp.concatenate([s_win.reshape(B, H, GRID_W, kh * GRID_W), s_meta], axis=-1)
        p = jax.nn.softmax(scores, axis=-1).astype(v.dtype)
        p_win = p[..., :kh * GRID_W].reshape(B, H, GRID_W, kh, GRID_W)
        p_meta = p[..., kh * GRID_W:]
        return (jnp.einsum('bhqrk,bhrkd->bhqd', p_win, v_win)
                + jnp.einsum('bhqm,bhmd->bhqd', p_meta, vm_h))

    out_g = lax.map(row_block, (row_ids, row_start, qg))
    out_g = out_g.transpose(1, 0, 3, 2, 4).reshape(B, N, H * d)
    s_mm = jnp.einsum('bmhd,bnhd->bhmn', qm, km).astype(jnp.float32) + meta_bias[None, :, None, :].astype(jnp.float32)
    p_mm = jax.nn.softmax(s_mm, axis=-1).astype(v.dtype)
    out_m = jnp.einsum('bhmn,bnhd->bmhd', p_mm, vm).reshape(B, N_META, H * d)
    return jnp.concatenate([out_m, out_g], axis=1)


def gla_chunked(q, k, v, g):
    B, H, T, dk = q.shape
    dv = v.shape[-1]
    nc = T // GLA_CHUNK
    qf = q.astype(jnp.float32).reshape(B, H, nc, GLA_CHUNK, dk)
    kf = k.astype(jnp.float32).reshape(B, H, nc, GLA_CHUNK, dk)
    vf = v.astype(jnp.float32).reshape(B, H, nc, GLA_CHUNK, dv)
    b = jnp.cumsum(g.astype(jnp.float32).reshape(B, H, nc, GLA_CHUNK, dk), axis=3)
    b_last = b[:, :, :, -1:]
    q_e = qf * jnp.exp(b)
    k_e = kf * jnp.exp(-b)
    k_d = kf * jnp.exp(b_last - b)
    causal = jnp.tril(jnp.ones((GLA_CHUNK, GLA_CHUNK), dtype=bool))
    a = jnp.where(causal, jnp.einsum('bhnik,bhnjk->bhnij', q_e, k_e), 0.0)
    o_intra = jnp.einsum('bhnij,bhnjv->bhniv', a, vf)
    kv = jnp.einsum('bhnjk,bhnjv->bhnkv', k_d, vf)
    decay = jnp.exp(b_last[:, :, :, 0])

    def step(state, inp):
        kv_n, dec_n = inp
        return dec_n[..., None] * state + kv_n, state

    init = jnp.zeros((B, H, dk, dv), jnp.float32)
    _, s_prev = lax.scan(step, init, (kv.transpose(2, 0, 1, 3, 4), decay.transpose(2, 0, 1, 3)))
    o_inter = jnp.einsum('bhnik,nbhkv->bhniv', q_e, s_prev)
    return (o_intra + o_inter).reshape(B, H, T, dv).astype(v.dtype)


def gla_bidirectional(q, k, v, r, gate_f_low, gate_b_low, w_gate_up_f, b_gate_f, w_gate_up_b, b_gate_b, norm_gain):
    B, L, _ = q.shape
    pad = (-N_META) % GLA_CHUNK
    g_f = jax.nn.log_sigmoid((gate_f_low @ w_gate_up_f + b_gate_f).astype(jnp.float32)) / GLA_GATE_TAU
    g_b = jax.nn.log_sigmoid((gate_b_low @ w_gate_up_b + b_gate_b).astype(jnp.float32)) / GLA_GATE_TAU

    def heads(a, dh):
        a = a.reshape(B, L, GLA_HEADS, dh).transpose(0, 2, 1, 3)
        return jnp.pad(a, ((0, 0), (0, 0), (pad, 0), (0, 0)))

    qh = heads(q * (GLA_DK ** -0.5), GLA_DK)
    kh = heads(k, GLA_DK)
    vh = heads(v, GLA_DV)
    gfh = heads(g_f, GLA_DK)
    gbh = heads(g_b, GLA_DK)
    flip = lambda a: jnp.flip(a, axis=2)
    o_fwd = gla_chunked(qh, kh, vh, gfh)
    o_bwd = flip(gla_chunked(flip(qh), flip(kh), flip(vh), flip(gbh)))
    o = (o_fwd + o_bwd)[:, :, pad:].transpose(0, 2, 1, 3)
    o = rms_norm(o, norm_gain).reshape(B, L, GLA_VAL_WIDTH)
    return o * jax.nn.silu(r)


def setup_inputs(seed: int = 0) -> dict:
    key = jax.random.key(seed)
    ks = jax.random.split(key, 20)
    f32 = jnp.float32
    nrm = lambda k, shape, scale: jax.random.normal(k, shape, f32) * scale
    gains = lambda k, shape: 1.0 + 0.05 * jax.random.normal(k, shape, f32)
    return {
        "x": jax.random.normal(ks[0], (BATCH, SEQ, D_MODEL), f32),
        "meta_tokens": nrm(ks[1], (N_META, D_MODEL), 1.0),
        "norm_mix_gain": gains(ks[2], (DEPTH, D_MODEL)),
        "w_in": nrm(ks[3], (DEPTH, D_MODEL, IN_WIDTH), D_MODEL ** -0.5),
        "rpb": nrm(ks[4], (DEPTH, NA_HEADS, 2 * NA_KH_MAX - 1, 2 * NA_KW - 1), 0.1),
        "meta_bias": nrm(ks[5], (DEPTH, NA_HEADS, N_META), 0.1),
        "w_gate_up_fwd": nrm(ks[6], (DEPTH, GLA_GATE_RANK, GLA_KEY_WIDTH), GLA_GATE_RANK ** -0.5),
        "b_gate_fwd": nrm(ks[7], (DEPTH, GLA_KEY_WIDTH), 0.1),
        "w_gate_up_bwd": nrm(ks[8], (DEPTH, GLA_GATE_RANK, GLA_KEY_WIDTH), GLA_GATE_RANK ** -0.5),
        "b_gate_bwd": nrm(ks[9], (DEPTH, GLA_KEY_WIDTH), 0.1),
        "gla_norm_gain": gains(ks[10], (DEPTH, GLA_DV)),
        "w_out": nrm(ks[11], (DEPTH, MIX_WIDTH, D_MODEL), MIX_WIDTH ** -0.5),
        "norm_ffn_gain": gains(ks[12], (DEPTH, D_MODEL)),
        "w_ffn_gate": nrm(ks[13], (DEPTH, D_MODEL, D_FF), D_MODEL ** -0.5),
        "w_ffn_up": nrm(ks[14], (DEPTH, D_MODEL, D_FF), D_MODEL ** -0.5),
        "w_ffn_down": nrm(ks[15], (DEPTH, D_FF, D_MODEL), D_FF ** -0.5),
        "norm_final_gain": gains(ks[16], (D_MODEL,)),
    }


def reference(x, meta_tokens, norm_mix_gain, w_in, rpb, meta_bias, w_gate_up_fwd, b_gate_fwd,
              w_gate_up_bwd, b_gate_bwd, gla_norm_gain, w_out, norm_ffn_gain, w_ffn_gate,
              w_ffn_up, w_ffn_down, norm_final_gain):
    B = x.shape[0]
    meta = jnp.broadcast_to(meta_tokens.astype(x.dtype)[None], (B, N_META, D_MODEL))
    h = jnp.concatenate([meta, x], axis=1)
    L = h.shape[1]
    split_points = [int(s) for s in np.cumsum(IN_SPLIT_SIZES)[:-1]]
    for layer in range(DEPTH):
        xn = rms_norm(h, norm_mix_gain[layer])
        proj = xn @ w_in[layer]
        na_q, na_k, na_v, g_q, g_k, g_v, g_r, g_fl, g_bl = jnp.split(proj, split_points, axis=-1)
        na_heads = lambda a: a.reshape(B, L, NA_HEADS, NA_HEAD_DIM)
        na_out = neighbourhood_attention(na_heads(na_q), na_heads(na_k), na_heads(na_v),
                                         rpb[layer], meta_bias[layer])
        gla_out = gla_bidirectional(g_q, g_k, g_v, g_r, g_fl, g_bl,
                                    w_gate_up_fwd[layer], b_gate_fwd[layer],
                                    w_gate_up_bwd[layer], b_gate_bwd[layer],
                                    gla_norm_gain[layer])
        h = h + jnp.concatenate([na_out, gla_out], axis=-1) @ w_out[layer]
        hn = rms_norm(h, norm_ffn_gain[layer])
        h = h + (jax.nn.silu(hn @ w_ffn_gate[layer]) * (hn @ w_ffn_up[layer])) @ w_ffn_down[layer]
    h = rms_norm(h, norm_final_gain)
    return h[:, N_META:]
```

```python
import functools

import jax
import jax.numpy as jnp
from jax import lax
from jax.experimental import pallas as pl
from jax.experimental.pallas import tpu as pltpu

F32 = jnp.float32
BF16 = jnp.bfloat16

N_META = 16
GRID_W = 64
NA_HEADS = 8
NA_HEAD_DIM = 64
NA_WIDTH = NA_HEADS * NA_HEAD_DIM
NA_KH_MAX = 8
NA_KW = 16
GLA_HEADS = 4
GLA_DK = 64
GLA_DV = 128
GLA_KEY_WIDTH = GLA_HEADS * GLA_DK
GLA_VAL_WIDTH = GLA_HEADS * GLA_DV
GLA_GATE_RANK = 16
GLA_GATE_TAU = 16.0
GLA_CHUNK = 64
RMS_EPS = 1e-6
NEG_INF = -1e30

LANES = 128
GATE_SLAB = LANES
VMEM_LIMIT = 48 * 1024 * 1024

_OFF_NA_Q = 0
_OFF_NA_K = _OFF_NA_Q + NA_WIDTH
_OFF_NA_V = _OFF_NA_K + NA_WIDTH
_OFF_G_Q = _OFF_NA_V + NA_WIDTH
_OFF_G_K = _OFF_G_Q + GLA_KEY_WIDTH
_OFF_G_V = _OFF_G_K + GLA_KEY_WIDTH
_OFF_G_R = _OFF_G_V + GLA_VAL_WIDTH
_OFF_GATE = _OFF_G_R + GLA_VAL_WIDTH
IN_WIDTH = _OFF_GATE + 2 * GLA_GATE_RANK
IN_WIDTH_PAD = _OFF_GATE + GATE_SLAB


def _rms(x, gain):
    ms = jnp.mean(x * x, axis=-1, keepdims=True)
    return x * lax.rsqrt(ms + RMS_EPS) * gain


def _silu(x):
    return x * (1.0 / (1.0 + jnp.exp(-x)))


def _log_sigmoid(x):
    return jnp.minimum(x, 0.0) - jnp.log1p(jnp.exp(-jnp.abs(x)))


def _dot(a, b):
    return jnp.dot(a, b, preferred_element_type=F32)


def _dot_nt(a, b):
    return lax.dot_general(a, b, (((1,), (1,)), ((), ())), preferred_element_type=F32)


def _dot_tn(a, b):
    return lax.dot_general(a, b, (((0,), (0,)), ((), ())), preferred_element_type=F32)


def _head_masks(rows, dtype):
    lane = lax.broadcasted_iota(jnp.int32, (rows, LANES), 1)
    lo = jnp.where(lane < LANES // 2, 1.0, 0.0).astype(dtype)
    hi = jnp.where(lane >= LANES // 2, 1.0, 0.0).astype(dtype)
    return lo, hi


def _in_proj_kernel(x_ref, gain_ref, w_ref, wupf_ref, bf_ref, wupb_ref, bb_ref,
                    naq_ref, nak_ref, nav_ref, gq_ref, gk_ref, gv_ref, gr_ref, gf_ref, gb_ref):
    xn = _rms(x_ref[...], gain_ref[...]).astype(BF16)

    def proj(lo, width):
        return _dot(xn, w_ref[:, lo:lo + width])

    naq_ref[...] = (proj(_OFF_NA_Q, NA_WIDTH) * (NA_HEAD_DIM ** -0.5)).astype(BF16)
    nak_ref[...] = proj(_OFF_NA_K, NA_WIDTH).astype(BF16)
    nav_ref[...] = proj(_OFF_NA_V, NA_WIDTH).astype(BF16)
    gq_ref[...] = proj(_OFF_G_Q, GLA_KEY_WIDTH) * (GLA_DK ** -0.5)
    gk_ref[...] = proj(_OFF_G_K, GLA_KEY_WIDTH)
    gv_ref[...] = proj(_OFF_G_V, GLA_VAL_WIDTH).astype(BF16)
    gr_ref[...] = _silu(proj(_OFF_G_R, GLA_VAL_WIDTH))
    low = proj(_OFF_GATE, GATE_SLAB).astype(BF16)
    gf_ref[...] = _log_sigmoid(_dot(low, wupf_ref[...]) + bf_ref[...]) * (1.0 / GLA_GATE_TAU)
    gb_ref[...] = _log_sigmoid(_dot(low, wupb_ref[...]) + bb_ref[...]) * (1.0 / GLA_GATE_TAU)


def _in_proj(x2d, gain, w_pad, wupf, bgf, wupb, bgb, tm):
    m, d = x2d.shape
    const = lambda shape: pl.BlockSpec(shape, lambda i: (0, 0))
    row = lambda width: pl.BlockSpec((tm, width), lambda i: (i, 0))
    widths = (NA_WIDTH, NA_WIDTH, NA_WIDTH, GLA_KEY_WIDTH, GLA_KEY_WIDTH,
              GLA_VAL_WIDTH, GLA_VAL_WIDTH, GLA_KEY_WIDTH, GLA_KEY_WIDTH)
    dtypes = (BF16, BF16, BF16, F32, F32, BF16, F32, F32, F32)
    return pl.pallas_call(
        _in_proj_kernel,
        grid=(m // tm,),
        in_specs=[row(d), const((1, d)), const(w_pad.shape), const(wupf.shape), const(bgf.shape),
                  const(wupb.shape), const(bgb.shape)],
        out_specs=[row(w) for w in widths],
        out_shape=[jax.ShapeDtypeStruct((m, w), dt) for w, dt in zip(widths, dtypes)],
        compiler_params=pltpu.CompilerParams(dimension_semantics=("arbitrary",),
                                             vmem_limit_bytes=VMEM_LIMIT),
        name="in_proj",
    )(x2d, gain, w_pad, wupf, bgf, wupb, bgb)


def _gla_init_kernel(gk_ref, gv_ref, gf_ref, st_ref):
    n = gk_ref.shape[0]
    r = lax.broadcasted_iota(jnp.int32, (n, n), 0)
    c = lax.broadcasted_iota(jnp.int32, (n, n), 1)
    later = jnp.where(c > r, 1.0, 0.0).astype(BF16)
    g = gf_ref[...]
    g_hi = g.astype(BF16)
    g_lo = (g - g_hi.astype(F32)).astype(BF16)
    tail = _dot(later, g_hi) + _dot(later, g_lo)
    k_d = (gk_ref[...] * jnp.exp(tail)).astype(BF16)
    for h in range(GLA_HEADS):
        p = h // 2
        st_ref[h] = _dot_tn(gv_ref[:, GLA_DV * h:GLA_DV * (h + 1)], k_d[:, LANES * p:LANES * (p + 1)])


def _gla_init(gk_m, gv_m, gf_m):
    return pl.pallas_call(
        _gla_init_kernel,
        out_shape=jax.ShapeDtypeStruct((GLA_HEADS, GLA_DV, LANES), F32),
        name="gla_init",
    )(gk_m, gv_m, gf_m)


def _na_kernel(q_ref, k_ref, v_ref, km_ref, vm_ref, bias_ref, mb_ref, o_ref, *, rows_per_step, n_rows):
    step = pl.program_id(1)
    kh = min(NA_KH_MAX, n_rows)
    masks = _head_masks(GRID_W, BF16)
    sel_lo = lax.broadcasted_iota(jnp.int32, (GRID_W, LANES), 1) < LANES // 2

    def row_body(r, carry):
        i = step * rows_per_step + r
        s = jnp.clip(i - kh // 2, 0, n_rows - kh)
        dr0 = s - i + (NA_KH_MAX - 1)
        qoff = pl.multiple_of(r * GRID_W, GRID_W)
        koff = pl.multiple_of(s * GRID_W, GRID_W)
        for p in range(NA_HEADS // 2):
            lanes = slice(LANES * p, LANES * (p + 1))
            q_pair = q_ref[0, pl.ds(qoff, GRID_W), lanes]
            k_win = k_ref[0, pl.ds(koff, kh * GRID_W), lanes]
            v_win = v_ref[0, pl.ds(koff, kh * GRID_W), lanes]
            k_meta = km_ref[:, lanes]
            v_meta = vm_ref[:, lanes]
            outs = []
            for hh in range(2):
                h = 2 * p + hh
                qm = q_pair * masks[hh]
                s_win = _dot_nt(qm, k_win)
                s_meta = _dot_nt(qm, k_meta) + mb_ref[h:h + 1, :]
                chunks = [s_win[:, LANES * t:LANES * (t + 1)] + bias_ref[h, dr0 + 2 * t]
                          for t in range(kh * GRID_W // LANES)]
                m_el = functools.reduce(jnp.maximum, chunks)
                m = jnp.maximum(jnp.max(m_el, axis=-1, keepdims=True),
                                jnp.max(s_meta, axis=-1, keepdims=True))
                p_chunks = [jnp.exp(c - m) for c in chunks]
                p_meta = jnp.exp(s_meta - m)
                l = (jnp.sum(functools.reduce(jnp.add, p_chunks), axis=-1, keepdims=True)
                     + jnp.sum(p_meta, axis=-1, keepdims=True))
                p_win = jnp.concatenate(p_chunks, axis=1).astype(BF16)
                o = _dot(p_win, v_win) + _dot(p_meta.astype(BF16), v_meta)
                outs.append(o / l)
            o_ref[0, pl.ds(qoff, GRID_W), lanes] = jnp.where(sel_lo, outs[0], outs[1]).astype(BF16)
        return carry

    lax.fori_loop(0, rows_per_step, row_body, 0)


def _na(q, k, v, k_meta, v_meta, bias_pairs, meta_bias, rows_per_step):
    b, n, w = q.shape
    n_rows = n // GRID_W
    kern = functools.partial(_na_kernel, rows_per_step=rows_per_step, n_rows=n_rows)
    tq = rows_per_step * GRID_W
    return pl.pallas_call(
        kern,
        grid=(b, n_rows // rows_per_step),
        in_specs=[
            pl.BlockSpec((1, tq, w), lambda bi, ri: (bi, ri, 0)),
            pl.BlockSpec((1, n, w), lambda bi, ri: (bi, 0, 0)),
            pl.BlockSpec((1, n, w), lambda bi, ri: (bi, 0, 0)),
            pl.BlockSpec(k_meta.shape, lambda bi, ri: (0, 0)),
            pl.BlockSpec(v_meta.shape, lambda bi, ri: (0, 0)),
            pl.BlockSpec(bias_pairs.shape, lambda bi, ri: (0, 0, 0, 0)),
            pl.BlockSpec(meta_bias.shape, lambda bi, ri: (0, 0)),
        ],
        out_specs=pl.BlockSpec((1, tq, w), lambda bi, ri: (bi, ri, 0)),
        out_shape=jax.ShapeDtypeStruct((b, n, w), BF16),
        compiler_params=pltpu.CompilerParams(dimension_semantics=("arbitrary", "arbitrary"),
                                             vmem_limit_bytes=VMEM_LIMIT),
        name="na",
    )(q, k, v, k_meta, v_meta, bias_pairs, meta_bias)


def _gla_kernel(gq_ref, gk_ref, gv_ref, gf_ref, gb_ref, gr_ref, s0_ref, gain_ref, o_ref,
                ofwd_ref, st_ref):
    n_chunks = gq_ref.shape[1] // GLA_CHUNK
    r = lax.broadcasted_iota(jnp.int32, (GLA_CHUNK, GLA_CHUNK), 0)
    c = lax.broadcasted_iota(jnp.int32, (GLA_CHUNK, GLA_CHUNK), 1)
    masks = _head_masks(GLA_CHUNK, F32)

    def chunk_outputs(n, g_ref, keep, last_row):
        rows = pl.ds(pl.multiple_of(n * GLA_CHUNK, GLA_CHUNK), GLA_CHUNK)
        tri = jnp.where(keep, 1.0, 0.0).astype(BF16)
        g = g_ref[0, rows, :]
        g_hi = g.astype(BF16)
        g_lo = (g - g_hi.astype(F32)).astype(BF16)
        b = _dot(tri, g_hi) + _dot(tri, g_lo)
        b_last = b[last_row:last_row + 1, :]
        q = gq_ref[0, rows, :]
        k = gk_ref[0, rows, :]
        q_e = q * jnp.exp(b)
        k_e = (k * jnp.exp(-b)).astype(BF16)
        k_d = (k * jnp.exp(b_last - b)).astype(BF16)
        decay = jnp.exp(b_last)
        outs = []
        for hh in range(2):
            qm = (q_e * masks[hh]).astype(BF16)
            a = jnp.where(keep, _dot_nt(qm, k_e), 0.0).astype(BF16)
            v_h = gv_ref[0, rows, GLA_DV * hh:GLA_DV * (hh + 1)]
            st = st_ref[hh]
            outs.append(_dot(a, v_h) + _dot_nt(qm, st.astype(BF16)))
            st_ref[hh] = decay * st + _dot_tn(v_h, k_d)
        return rows, outs

    st_ref[...] = s0_ref[...]

    def fwd_body(n, carry):
        rows, outs = chunk_outputs(n, gf_ref, c <= r, GLA_CHUNK - 1)
        for hh in range(2):
            ofwd_ref[rows, GLA_DV * hh:GLA_DV * (hh + 1)] = outs[hh]
        return carry

    lax.fori_loop(0, n_chunks, fwd_body, 0)

    st_ref[...] = jnp.zeros_like(st_ref)

    def bwd_body(t, carry):
        n = n_chunks - 1 - t
        rows, outs = chunk_outputs(n, gb_ref, c >= r, 0)
        for hh in range(2):
            cols = slice(GLA_DV * hh, GLA_DV * (hh + 1))
            o = _rms(ofwd_ref[rows, cols] + outs[hh], gain_ref[...])
            o_ref[0, rows, cols] = (o * gr_ref[0, rows, cols]).astype(BF16)
        return carry

    lax.fori_loop(0, n_chunks, bwd_body, 0)


def _gla(gq, gk, gv, gf, gb, gr, s0, gain):
    b, n, _ = gq.shape
    n_pairs = GLA_HEADS // 2
    key_spec = pl.BlockSpec((1, n, LANES), lambda bi, pi: (bi, 0, pi))
    val_spec = pl.BlockSpec((1, n, 2 * GLA_DV), lambda bi, pi: (bi, 0, pi))
    return pl.pallas_call(
        _gla_kernel,
        grid=(b, n_pairs),
        in_specs=[key_spec, key_spec, val_spec, key_spec, key_spec, val_spec,
                  pl.BlockSpec((2, GLA_DV, LANES), lambda bi, pi: (pi, 0, 0)),
                  pl.BlockSpec((1, GLA_DV), lambda bi, pi: (0, 0))],
        out_specs=val_spec,
        out_shape=jax.ShapeDtypeStruct((b, n, GLA_VAL_WIDTH), BF16),
        scratch_shapes=[pltpu.VMEM((n, 2 * GLA_DV), F32), pltpu.VMEM((2, GLA_DV, LANES), F32)],
        compiler_params=pltpu.CompilerParams(dimension_semantics=("arbitrary", "arbitrary"),
                                             vmem_limit_bytes=VMEM_LIMIT),
        name="gla",
    )(gq, gk, gv, gf, gb, gr, s0, gain)


def _out_ffn_kernel(x_ref, na_ref, gla_ref, wout_ref, gffn_ref, wg_ref, wu_ref, wd_ref, gfin_ref, o_ref):
    mix = _dot(na_ref[...], wout_ref[0:NA_WIDTH, :]) + _dot(gla_ref[...], wout_ref[NA_WIDTH:, :])
    h = x_ref[...] + mix
    hn = _rms(h, gffn_ref[...]).astype(BF16)
    act = (_silu(_dot(hn, wg_ref[...])) * _dot(hn, wu_ref[...])).astype(BF16)
    h = h + _dot(act, wd_ref[...])
    o_ref[...] = _rms(h, gfin_ref[...])


def _out_ffn(x2d, na2d, gla2d, w_out, g_ffn, w_gate, w_up, w_down, g_fin, tm):
    m, d = x2d.shape
    const = lambda shape: pl.BlockSpec(shape, lambda i: (0, 0))
    row = lambda width: pl.BlockSpec((tm, width), lambda i: (i, 0))
    return pl.pallas_call(
        _out_ffn_kernel,
        grid=(m // tm,),
        in_specs=[row(d), row(NA_WIDTH), row(GLA_VAL_WIDTH), const(w_out.shape), const((1, d)),
                  const(w_gate.shape), const(w_up.shape), const(w_down.shape), const((1, d))],
        out_specs=row(d),
        out_shape=jax.ShapeDtypeStruct((m, d), F32),
        compiler_params=pltpu.CompilerParams(dimension_semantics=("arbitrary",),
                                             vmem_limit_bytes=56 * 1024 * 1024),
        name="out_ffn",
    )(x2d, na2d, gla2d, w_out, g_ffn, w_gate, w_up, w_down, g_fin)


def _na_bias_pairs(rpb):
    cols = jnp.arange(GRID_W)
    col_start = jnp.clip(cols - NA_KW // 2, 0, GRID_W - NA_KW)
    valid = (cols[None, :] >= col_start[:, None]) & (cols[None, :] < col_start[:, None] + NA_KW)
    dc = jnp.clip(cols[None, :] - cols[:, None], -(NA_KW - 1), NA_KW - 1) + (NA_KW - 1)
    tiles = jnp.where(valid[None, None], rpb[:, :, dc], NEG_INF).astype(F32)
    return jnp.concatenate([tiles[:, :-1], tiles[:, 1:]], axis=-1)


def kernel(x, meta_tokens, norm_mix_gain, w_in, rpb, meta_bias, w_gate_up_fwd, b_gate_fwd, w_gate_up_bwd, b_gate_bwd, gla_norm_gain, w_out, norm_ffn_gain, w_ffn_gate, w_ffn_up, w_ffn_down, norm_final_gain):
    bsz, seq, d = x.shape
    assert w_in.shape[0] == 1, "single-layer block: meta-token outputs are never consumed"
    assert w_in.shape[2] == IN_WIDTH and seq % GRID_W == 0 and GLA_CHUNK == GRID_W

    w_pad = jnp.concatenate([w_in[0], jnp.zeros((d, IN_WIDTH_PAD - IN_WIDTH), F32)], axis=1).astype(BF16)
    zeros_up = jnp.zeros((GATE_SLAB - 2 * GLA_GATE_RANK, GLA_KEY_WIDTH), F32)
    wupf = jnp.concatenate([w_gate_up_fwd[0], jnp.zeros_like(w_gate_up_bwd[0]), zeros_up], axis=0).astype(BF16)
    wupb = jnp.concatenate([jnp.zeros_like(w_gate_up_fwd[0]), w_gate_up_bwd[0], zeros_up], axis=0).astype(BF16)
    bgf = b_gate_fwd[0][None, :]
    bgb = b_gate_bwd[0][None, :]
    gain_mix = norm_mix_gain[0][None, :]

    x2d = x.reshape(bsz * seq, d)
    proj_args = (gain_mix, w_pad, wupf, bgf, wupb, bgb)
    naq, nak, nav, gq, gk, gv, gr, gf, gb = _in_proj(x2d, *proj_args, tm=512)
    _, km, vm, _, gk_m, gv_m, _, gf_m, _ = _in_proj(meta_tokens, *proj_args, tm=N_META)

    s0 = _gla_init(gk_m, gv_m, gf_m)

    per_batch = lambda a: a.reshape(bsz, seq, a.shape[-1])
    na_out = _na(per_batch(naq), per_batch(nak), per_batch(nav), km, vm,
                 _na_bias_pairs(rpb[0]), meta_bias[0], rows_per_step=8)
    gla_out = _gla(per_batch(gq), per_batch(gk), per_batch(gv), per_batch(gf), per_batch(gb),
                   per_batch(gr), s0, gla_norm_gain[0][None, :])

    out = _out_ffn(x2d, na_out.reshape(bsz * seq, NA_WIDTH), gla_out.reshape(bsz * seq, GLA_VAL_WIDTH),
                   w_out[0].astype(BF16), norm_ffn_gain[0][None, :], w_ffn_gate[0].astype(BF16),
                   w_ffn_up[0].astype(BF16), w_ffn_down[0].astype(BF16), norm_final_gain[None, :], tm=256)
    return out.reshape(bsz, seq, d)
```

```python
import functools

import jax
import jax.numpy as jnp
from jax import lax
from jax.experimental import pallas as pl
from jax.experimental.pallas import tpu as pltpu

F32 = jnp.float32
BF16 = jnp.bfloat16

N_META = 16
GRID_W = 64
NA_HEADS = 8
NA_HEAD_DIM = 64
NA_WIDTH = NA_HEADS * NA_HEAD_DIM
NA_KH_MAX = 8
NA_KW = 16
GLA_HEADS = 4
GLA_DK = 64
GLA_DV = 128
GLA_KEY_WIDTH = GLA_HEADS * GLA_DK
GLA_VAL_WIDTH = GLA_HEADS * GLA_DV
GLA_GATE_RANK = 16
GLA_GATE_TAU = 16.0
GLA_CHUNK = 64
RMS_EPS = 1e-6
NEG_INF = -1e30

LANES = 128
GATE_SLAB = LANES
VMEM_LIMIT = 48 * 1024 * 1024

_OFF_NA_Q = 0
_OFF_NA_K = _OFF_NA_Q + NA_WIDTH
_OFF_NA_V = _OFF_NA_K + NA_WIDTH
_OFF_G_Q = _OFF_NA_V + NA_WIDTH
_OFF_G_K = _OFF_G_Q + GLA_KEY_WIDTH
_OFF_G_V = _OFF_G_K + GLA_KEY_WIDTH
_OFF_G_R = _OFF_G_V + GLA_VAL_WIDTH
_OFF_GATE = _OFF_G_R + GLA_VAL_WIDTH
IN_WIDTH = _OFF_GATE + 2 * GLA_GATE_RANK
IN_WIDTH_PAD = _OFF_GATE + GATE_SLAB


def _rms(x, gain):
    ms = jnp.mean(x * x, axis=-1, keepdims=True)
    return x * lax.rsqrt(ms + RMS_EPS) * gain


def _silu(x):
    return x * (1.0 / (1.0 + jnp.exp(-x)))


def _log_sigmoid(x):
    return jnp.minimum(x, 0.0) - jnp.log1p(jnp.exp(-jnp.abs(x)))


def _dot(a, b):
    return jnp.dot(a, b, preferred_element_type=F32)


def _dot_nt(a, b):
    return lax.dot_general(a, b, (((1,), (1,)), ((), ())), preferred_element_type=F32)


def _dot_tn(a, b):
    return lax.dot_general(a, b, (((0,), (0,)), ((), ())), preferred_element_type=F32)


def _head_masks(rows, dtype):
    lane = lax.broadcasted_iota(jnp.int32, (rows, LANES), 1)
    lo = jnp.where(lane < LANES // 2, 1.0, 0.0).astype(dtype)
    hi = jnp.where(lane >= LANES // 2, 1.0, 0.0).astype(dtype)
    return lo, hi


def _in_proj_kernel(x_ref, gain_ref, w_ref, wupf_ref, bf_ref, wupb_ref, bb_ref,
                    naq_ref, nak_ref, nav_ref, gq_ref, gk_ref, gv_ref, gr_ref, gf_ref, gb_ref):
    xn = _rms(x_ref[...], gain_ref[...]).astype(BF16)

    def proj(lo, width):
        return _dot(xn, w_ref[:, lo:lo + width])

    naq_ref[...] = (proj(_OFF_NA_Q, NA_WIDTH) * (NA_HEAD_DIM ** -0.5)).astype(BF16)
    nak_ref[...] = proj(_OFF_NA_K, NA_WIDTH).astype(BF16)
    nav_ref[...] = proj(_OFF_NA_V, NA_WIDTH).astype(BF16)
    gq_ref[...] = proj(_OFF_G_Q, GLA_KEY_WIDTH) * (GLA_DK ** -0.5)
    gk_ref[...] = proj(_OFF_G_K, GLA_KEY_WIDTH)
    gv_ref[...] = proj(_OFF_G_V, GLA_VAL_WIDTH).astype(BF16)
    gr_ref[...] = _silu(proj(_OFF_G_R, GLA_VAL_WIDTH))
    low = proj(_OFF_GATE, GATE_SLAB).astype(BF16)
    gf_ref[...] = _log_sigmoid(_dot(low, wupf_ref[...]) + bf_ref[...]) * (1.0 / GLA_GATE_TAU)
    gb_ref[...] = _log_sigmoid(_dot(low, wupb_ref[...]) + bb_ref[...]) * (1.0 / GLA_GATE_TAU)


def _in_proj(x2d, gain, w_pad, wupf, bgf, wupb, bgb, tm):
    m, d = x2d.shape
    const = lambda shape: pl.BlockSpec(shape, lambda i: (0, 0))
    row = lambda width: pl.BlockSpec((tm, width), lambda i: (i, 0))
    widths = (NA_WIDTH, NA_WIDTH, NA_WIDTH, GLA_KEY_WIDTH, GLA_KEY_WIDTH,
              GLA_VAL_WIDTH, GLA_VAL_WIDTH, GLA_KEY_WIDTH, GLA_KEY_WIDTH)
    dtypes = (BF16, BF16, BF16, F32, F32, BF16, F32, F32, F32)
    return pl.pallas_call(
        _in_proj_kernel,
        grid=(m // tm,),
        in_specs=[row(d), const((1, d)), const(w_pad.shape), const(wupf.shape), const(bgf.shape),
                  const(wupb.shape), const(bgb.shape)],
        out_specs=[row(w) for w in widths],
        out_shape=[jax.ShapeDtypeStruct((m, w), dt) for w, dt in zip(widths, dtypes)],
        compiler_params=pltpu.CompilerParams(dimension_semantics=("arbitrary",),
                                             vmem_limit_bytes=VMEM_LIMIT),
        name="in_proj",
    )(x2d, gain, w_pad, wupf, bgf, wupb, bgb)


def _gla_init_kernel(gk_ref, gv_ref, gf_ref, st_ref):
    n = gk_ref.shape[0]
    r = lax.broadcasted_iota(jnp.int32, (n, n), 0)
    c = lax.broadcasted_iota(jnp.int32, (n, n), 1)
    later = jnp.where(c > r, 1.0, 0.0).astype(BF16)
    g = gf_ref[...]
    g_hi = g.astype(BF16)
    g_lo = (g - g_hi.astype(F32)).astype(BF16)
    tail = _dot(later, g_hi) + _dot(later, g_lo)
    k_d = (gk_ref[...] * jnp.exp(tail)).astype(BF16)
    for h in range(GLA_HEADS):
        p = h // 2
        st_ref[h] = _dot_tn(gv_ref[:, GLA_DV * h:GLA_DV * (h + 1)], k_d[:, LANES * p:LANES * (p + 1)])


def _gla_init(gk_m, gv_m, gf_m):
    return pl.pallas_call(
        _gla_init_kernel,
        out_shape=jax.ShapeDtypeStruct((GLA_HEADS, GLA_DV, LANES), F32),
        name="gla_init",
    )(gk_m, gv_m, gf_m)


def _na_kernel(q_ref, k_ref, v_ref, km_ref, vm_ref, bias_ref, mb_ref, o_ref, *, rows_per_step, n_rows):
    step = pl.program_id(1)
    kh = min(NA_KH_MAX, n_rows)
    masks = _head_masks(GRID_W, BF16)
    sel_lo = lax.broadcasted_iota(jnp.int32, (GRID_W, LANES), 1) < LANES // 2

    def row_body(r, carry):
        i = step * rows_per_step + r
        s = jnp.clip(i - kh // 2, 0, n_rows - kh)
        dr0 = s - i + (NA_KH_MAX - 1)
        qoff = pl.multiple_of(r * GRID_W, GRID_W)
        koff = pl.multiple_of(s * GRID_W, GRID_W)
        n_pairs = NA_HEADS // 2
        pair_lanes = [slice(LANES * p, LANES * (p + 1)) for p in range(n_pairs)]
        scores = []
        for p in range(n_pairs):
            q_pair = q_ref[0, pl.ds(qoff, GRID_W), pair_lanes[p]]
            q2 = jnp.concatenate([q_pair * masks[0], q_pair * masks[1]], axis=0)
            k_win = k_ref[0, pl.ds(koff, kh * GRID_W), pair_lanes[p]]
            scores.append((_dot_nt(q2, k_win), _dot_nt(q2, km_ref[:, pair_lanes[p]])))
        pair_outs = []
        for p in range(n_pairs):
            s_win, s_meta = scores[p]
            s_meta = s_meta + mb_ref[p]
            chunks = [s_win[:, LANES * t:LANES * (t + 1)] + bias_ref[p, dr0 + 2 * t]
                      for t in range(kh * GRID_W // LANES)]
            m_el = functools.reduce(jnp.maximum, chunks)
            m = jnp.maximum(jnp.max(m_el, axis=-1, keepdims=True),
                            jnp.max(s_meta, axis=-1, keepdims=True))
            p_chunks = [jnp.exp(c - m) for c in chunks]
            p_meta = jnp.exp(s_meta - m)
            l = (jnp.sum(functools.reduce(jnp.add, p_chunks), axis=-1, keepdims=True)
                 + jnp.sum(p_meta, axis=-1, keepdims=True))
            p_win = jnp.concatenate(p_chunks, axis=1).astype(BF16)
            v_win = v_ref[0, pl.ds(koff, kh * GRID_W), pair_lanes[p]]
            o2 = (_dot(p_win, v_win) + _dot(p_meta.astype(BF16), vm_ref[:, pair_lanes[p]])) / l
            pair_outs.append(jnp.where(sel_lo, o2[:GRID_W], o2[GRID_W:]).astype(BF16))
        o_ref[0, pl.ds(qoff, GRID_W), :] = jnp.concatenate(pair_outs, axis=1)
        return carry

    lax.fori_loop(0, rows_per_step, row_body, 0)


def _na(q, k, v, k_meta, v_meta, bias_pairs, meta_bias, rows_per_step):
    b, n, w = q.shape
    n_rows = n // GRID_W
    kern = functools.partial(_na_kernel, rows_per_step=rows_per_step, n_rows=n_rows)
    tq = rows_per_step * GRID_W
    return pl.pallas_call(
        kern,
        grid=(b, n_rows // rows_per_step),
        in_specs=[
            pl.BlockSpec((1, tq, w), lambda bi, ri: (bi, ri, 0)),
            pl.BlockSpec((1, n, w), lambda bi, ri: (bi, 0, 0)),
            pl.BlockSpec((1, n, w), lambda bi, ri: (bi, 0, 0)),
            pl.BlockSpec(k_meta.shape, lambda bi, ri: (0, 0)),
            pl.BlockSpec(v_meta.shape, lambda bi, ri: (0, 0)),
            pl.BlockSpec(bias_pairs.shape, lambda bi, ri: (0, 0, 0, 0)),
            pl.BlockSpec(meta_bias.shape, lambda bi, ri: (0, 0, 0)),
        ],
        out_specs=pl.BlockSpec((1, tq, w), lambda bi, ri: (bi, ri, 0)),
        out_shape=jax.ShapeDtypeStruct((b, n, w), BF16),
        compiler_params=pltpu.CompilerParams(dimension_semantics=("arbitrary", "arbitrary"),
                                             vmem_limit_bytes=VMEM_LIMIT),
        name="na",
    )(q, k, v, k_meta, v_meta, bias_pairs, meta_bias)


def _gla_kernel(gq_ref, gk_ref, gv_ref, gf_ref, gb_ref, gr_ref, s0_ref, gain_ref, o_ref,
                ofwd_ref, st_ref):
    n_chunks = gq_ref.shape[1] // GLA_CHUNK
    r = lax.broadcasted_iota(jnp.int32, (GLA_CHUNK, GLA_CHUNK), 0)
    c = lax.broadcasted_iota(jnp.int32, (GLA_CHUNK, GLA_CHUNK), 1)
    masks = _head_masks(GLA_CHUNK, F32)

    def chunk_outputs(n, g_ref, keep, last_row):
        rows = pl.ds(pl.multiple_of(n * GLA_CHUNK, GLA_CHUNK), GLA_CHUNK)
        tri = jnp.where(keep, 1.0, 0.0).astype(BF16)
        g = g_ref[0, rows, :]
        g_hi = g.astype(BF16)
        g_lo = (g - g_hi.astype(F32)).astype(BF16)
        b = _dot(tri, g_hi) + _dot(tri, g_lo)
        b_last = b[last_row:last_row + 1, :]
        q = gq_ref[0, rows, :]
        k = gk_ref[0, rows, :]
        q_e = q * jnp.exp(b)
        k_e = (k * jnp.exp(-b)).astype(BF16)
        k_d = (k * jnp.exp(b_last - b)).astype(BF16)
        decay = jnp.exp(b_last)
        outs = []
        for hh in range(2):
            qm = (q_e * masks[hh]).astype(BF16)
            a = jnp.where(keep, _dot_nt(qm, k_e), 0.0).astype(BF16)
            v_h = gv_ref[0, rows, GLA_DV * hh:GLA_DV * (hh + 1)]
            st = st_ref[hh]
            outs.append(_dot(a, v_h) + _dot_nt(qm, st.astype(BF16)))
            st_ref[hh] = decay * st + _dot_tn(v_h, k_d)
        return rows, outs

    st_ref[...] = s0_ref[...]

    def fwd_body(n, carry):
        rows, outs = chunk_outputs(n, gf_ref, c <= r, GLA_CHUNK - 1)
        for hh in range(2):
            ofwd_ref[rows, GLA_DV * hh:GLA_DV * (hh + 1)] = outs[hh]
        return carry

    lax.fori_loop(0, n_chunks, fwd_body, 0)

    st_ref[...] = jnp.zeros_like(st_ref)

    def bwd_body(t, carry):
        n = n_chunks - 1 - t
        rows, outs = chunk_outputs(n, gb_ref, c >= r, 0)
        for hh in range(2):
            cols = slice(GLA_DV * hh, GLA_DV * (hh + 1))
            o = _rms(ofwd_ref[rows, cols] + outs[hh], gain_ref[...])
            o_ref[0, rows, cols] = (o * gr_ref[0, rows, cols]).astype(BF16)
        return carry

    lax.fori_loop(0, n_chunks, bwd_body, 0)


def _gla(gq, gk, gv, gf, gb, gr, s0, gain):
    b, n, _ = gq.shape
    n_pairs = GLA_HEADS // 2
    key_spec = pl.BlockSpec((1, n, LANES), lambda bi, pi: (bi, 0, pi))
    val_spec = pl.BlockSpec((1, n, 2 * GLA_DV), lambda bi, pi: (bi, 0, pi))
    return pl.pallas_call(
        _gla_kernel,
        grid=(b, n_pairs),
        in_specs=[key_spec, key_spec, val_spec, key_spec, key_spec, val_spec,
                  pl.BlockSpec((2, GLA_DV, LANES), lambda bi, pi: (pi, 0, 0)),
                  pl.BlockSpec((1, GLA_DV), lambda bi, pi: (0, 0))],
        out_specs=val_spec,
        out_shape=jax.ShapeDtypeStruct((b, n, GLA_VAL_WIDTH), BF16),
        scratch_shapes=[pltpu.VMEM((n, 2 * GLA_DV), F32), pltpu.VMEM((2, GLA_DV, LANES), F32)],
        compiler_params=pltpu.CompilerParams(dimension_semantics=("arbitrary", "arbitrary"),
                                             vmem_limit_bytes=VMEM_LIMIT),
        name="gla",
    )(gq, gk, gv, gf, gb, gr, s0, gain)


def _out_ffn_kernel(x_ref, na_ref, gla_ref, wout_ref, gffn_ref, wg_ref, wu_ref, wd_ref, gfin_ref, o_ref):
    mix = _dot(na_ref[...], wout_ref[0:NA_WIDTH, :]) + _dot(gla_ref[...], wout_ref[NA_WIDTH:, :])
    h = x_ref[...] + mix
    hn = _rms(h, gffn_ref[...]).astype(BF16)
    act = (_silu(_dot(hn, wg_ref[...])) * _dot(hn, wu_ref[...])).astype(BF16)
    h = h + _dot(act, wd_ref[...])
    o_ref[...] = _rms(h, gfin_ref[...])


def _out_ffn(x2d, na2d, gla2d, w_out, g_ffn, w_gate, w_up, w_down, g_fin, tm):
    m, d = x2d.shape
    const = lambda shape: pl.BlockSpec(shape, lambda i: (0, 0))
    row = lambda width: pl.BlockSpec((tm, width), lambda i: (i, 0))
    return pl.pallas_call(
        _out_ffn_kernel,
        grid=(m // tm,),
        in_specs=[row(d), row(NA_WIDTH), row(GLA_VAL_WIDTH), const(w_out.shape), const((1, d)),
                  const(w_gate.shape), const(w_up.shape), const(w_down.shape), const((1, d))],
        out_specs=row(d),
        out_shape=jax.ShapeDtypeStruct((m, d), F32),
        compiler_params=pltpu.CompilerParams(dimension_semantics=("arbitrary",),
                                             vmem_limit_bytes=56 * 1024 * 1024),
        name="out_ffn",
    )(x2d, na2d, gla2d, w_out, g_ffn, w_gate, w_up, w_down, g_fin)


def _na_bias_pairs(rpb):
    cols = jnp.arange(GRID_W)
    col_start = jnp.clip(cols - NA_KW // 2, 0, GRID_W - NA_KW)
    valid = (cols[None, :] >= col_start[:, None]) & (cols[None, :] < col_start[:, None] + NA_KW)
    dc = jnp.clip(cols[None, :] - cols[:, None], -(NA_KW - 1), NA_KW - 1) + (NA_KW - 1)
    tiles = jnp.where(valid[None, None], rpb[:, :, dc], NEG_INF).astype(F32)
    two_rows = jnp.concatenate([tiles[:, :-1], tiles[:, 1:]], axis=-1)
    n_dr = two_rows.shape[1]
    stacked = two_rows.reshape(NA_HEADS // 2, 2, n_dr, GRID_W, LANES).transpose(0, 2, 1, 3, 4)
    return stacked.reshape(NA_HEADS // 2, n_dr, 2 * GRID_W, LANES)


def _na_meta_bias_pairs(meta_bias):
    return jnp.repeat(meta_bias.reshape(NA_HEADS // 2, 2, N_META), GRID_W, axis=1)


def kernel(x, meta_tokens, norm_mix_gain, w_in, rpb, meta_bias, w_gate_up_fwd, b_gate_fwd, w_gate_up_bwd, b_gate_bwd, gla_norm_gain, w_out, norm_ffn_gain, w_ffn_gate, w_ffn_up, w_ffn_down, norm_final_gain):
    bsz, seq, d = x.shape
    assert w_in.shape[0] == 1, "single-layer block: meta-token outputs are never consumed"
    assert w_in.shape[2] == IN_WIDTH and seq % GRID_W == 0 and GLA_CHUNK == GRID_W

    w_pad = jnp.concatenate([w_in[0], jnp.zeros((d, IN_WIDTH_PAD - IN_WIDTH), F32)], axis=1).astype(BF16)
    zeros_up = jnp.zeros((GATE_SLAB - 2 * GLA_GATE_RANK, GLA_KEY_WIDTH), F32)
    wupf = jnp.concatenate([w_gate_up_fwd[0], jnp.zeros_like(w_gate_up_bwd[0]), zeros_up], axis=0).astype(BF16)
    wupb = jnp.concatenate([jnp.zeros_like(w_gate_up_fwd[0]), w_gate_up_bwd[0], zeros_up], axis=0).astype(BF16)
    bgf = b_gate_fwd[0][None, :]
    bgb = b_gate_bwd[0][None, :]
    gain_mix = norm_mix_gain[0][None, :]

    x2d = x.reshape(bsz * seq, d)
    proj_args = (gain_mix, w_pad, wupf, bgf, wupb, bgb)
    naq, nak, nav, gq, gk, gv, gr, gf, gb = _in_proj(x2d, *proj_args, tm=512)
    _, km, vm, _, gk_m, gv_m, _, gf_m, _ = _in_proj(meta_tokens, *proj_args, tm=N_META)

    s0 = _gla_init(gk_m, gv_m, gf_m)

    per_batch = lambda a: a.reshape(bsz, seq, a.shape[-1])
    na_out = _na(per_batch(naq), per_batch(nak), per_batch(nav), km, vm,
                 _na_bias_pairs(rpb[0]), _na_meta_bias_pairs(meta_bias[0]), rows_per_step=8)
    gla_out = _gla(per_batch(gq), per_batch(gk), per_batch(gv), per_batch(gf), per_batch(gb),
                   per_batch(gr), s0, gla_norm_gain[0][None, :])

    out = _out_ffn(x2d, na_out.reshape(bsz * seq, NA_WIDTH), gla_out.reshape(bsz * seq, GLA_VAL_WIDTH),
                   w_out[0].astype(BF16), norm_ffn_gain[0][None, :], w_ffn_gate[0].astype(BF16),
                   w_ffn_up[0].astype(BF16), w_ffn_down[0].astype(BF16), norm_final_gain[None, :], tm=256)
    return out.reshape(bsz, seq, d)
```

```python
import functools

import jax
import jax.numpy as jnp
from jax import lax
from jax.experimental import pallas as pl
from jax.experimental.pallas import tpu as pltpu

F32 = jnp.float32
BF16 = jnp.bfloat16

N_META = 16
GRID_W = 64
NA_HEADS = 8
NA_HEAD_DIM = 64
NA_WIDTH = NA_HEADS * NA_HEAD_DIM
NA_KH_MAX = 8
NA_KW = 16
GLA_HEADS = 4
GLA_DK = 64
GLA_DV = 128
GLA_KEY_WIDTH = GLA_HEADS * GLA_DK
GLA_VAL_WIDTH = GLA_HEADS * GLA_DV
GLA_GATE_RANK = 16
GLA_GATE_TAU = 16.0
GLA_CHUNK = 64
RMS_EPS = 1e-6
NEG_INF = -1e30

LANES = 128
GATE_SLAB = LANES
VMEM_LIMIT = 48 * 1024 * 1024

_OFF_NA_Q = 0
_OFF_NA_K = _OFF_NA_Q + NA_WIDTH
_OFF_NA_V = _OFF_NA_K + NA_WIDTH
_OFF_G_Q = _OFF_NA_V + NA_WIDTH
_OFF_G_K = _OFF_G_Q + GLA_KEY_WIDTH
_OFF_G_V = _OFF_G_K + GLA_KEY_WIDTH
_OFF_G_R = _OFF_G_V + GLA_VAL_WIDTH
_OFF_GATE = _OFF_G_R + GLA_VAL_WIDTH
IN_WIDTH = _OFF_GATE + 2 * GLA_GATE_RANK
IN_WIDTH_PAD = _OFF_GATE + GATE_SLAB


def _rms(x, gain):
    ms = jnp.mean(x * x, axis=-1, keepdims=True)
    return x * lax.rsqrt(ms + RMS_EPS) * gain


def _silu(x):
    return x * (1.0 / (1.0 + jnp.exp(-x)))


def _log_sigmoid(x):
    return jnp.minimum(x, 0.0) - jnp.log1p(jnp.exp(-jnp.abs(x)))


def _dot(a, b):
    return jnp.dot(a, b, preferred_element_type=F32)


def _dot_nt(a, b):
    return lax.dot_general(a, b, (((1,), (1,)), ((), ())), preferred_element_type=F32)


def _dot_tn(a, b):
    return lax.dot_general(a, b, (((0,), (0,)), ((), ())), preferred_element_type=F32)


def _head_masks(rows, dtype):
    lane = lax.broadcasted_iota(jnp.int32, (rows, LANES), 1)
    lo = jnp.where(lane < LANES // 2, 1.0, 0.0).astype(dtype)
    hi = jnp.where(lane >= LANES // 2, 1.0, 0.0).astype(dtype)
    return lo, hi


def _in_proj_kernel(x_ref, gain_ref, w_ref, wupf_ref, bf_ref, wupb_ref, bb_ref,
                    naq_ref, nak_ref, nav_ref, gq_ref, gk_ref, gv_ref, gr_ref, gf_ref, gb_ref):
    xn = _rms(x_ref[...], gain_ref[...]).astype(BF16)

    def proj(lo, width):
        return _dot(xn, w_ref[:, lo:lo + width])

    naq_ref[...] = (proj(_OFF_NA_Q, NA_WIDTH) * (NA_HEAD_DIM ** -0.5)).astype(BF16)
    nak_ref[...] = proj(_OFF_NA_K, NA_WIDTH).astype(BF16)
    nav_ref[...] = proj(_OFF_NA_V, NA_WIDTH).astype(BF16)
    gq_ref[...] = proj(_OFF_G_Q, GLA_KEY_WIDTH) * (GLA_DK ** -0.5)
    gk_ref[...] = proj(_OFF_G_K, GLA_KEY_WIDTH)
    gv_ref[...] = proj(_OFF_G_V, GLA_VAL_WIDTH).astype(BF16)
    gr_ref[...] = _silu(proj(_OFF_G_R, GLA_VAL_WIDTH))
    low = proj(_OFF_GATE, GATE_SLAB).astype(BF16)
    gf_ref[...] = _log_sigmoid(_dot(low, wupf_ref[...]) + bf_ref[...]) * (1.0 / GLA_GATE_TAU)
    gb_ref[...] = _log_sigmoid(_dot(low, wupb_ref[...]) + bb_ref[...]) * (1.0 / GLA_GATE_TAU)


def _in_proj(x2d, gain, w_pad, wupf, bgf, wupb, bgb, tm):
    m, d = x2d.shape
    const = lambda shape: pl.BlockSpec(shape, lambda i: (0, 0))
    row = lambda width: pl.BlockSpec((tm, width), lambda i: (i, 0))
    widths = (NA_WIDTH, NA_WIDTH, NA_WIDTH, GLA_KEY_WIDTH, GLA_KEY_WIDTH,
              GLA_VAL_WIDTH, GLA_VAL_WIDTH, GLA_KEY_WIDTH, GLA_KEY_WIDTH)
    dtypes = (BF16, BF16, BF16, F32, F32, BF16, F32, F32, F32)
    return pl.pallas_call(
        _in_proj_kernel,
        grid=(m // tm,),
        in_specs=[row(d), const((1, d)), const(w_pad.shape), const(wupf.shape), const(bgf.shape),
                  const(wupb.shape), const(bgb.shape)],
        out_specs=[row(w) for w in widths],
        out_shape=[jax.ShapeDtypeStruct((m, w), dt) for w, dt in zip(widths, dtypes)],
        compiler_params=pltpu.CompilerParams(dimension_semantics=("arbitrary",),
                                             vmem_limit_bytes=VMEM_LIMIT),
        name="in_proj",
    )(x2d, gain, w_pad, wupf, bgf, wupb, bgb)


def _gla_init_kernel(gk_ref, gv_ref, gf_ref, st_ref):
    n = gk_ref.shape[0]
    r = lax.broadcasted_iota(jnp.int32, (n, n), 0)
    c = lax.broadcasted_iota(jnp.int32, (n, n), 1)
    later = jnp.where(c > r, 1.0, 0.0).astype(BF16)
    g = gf_ref[...]
    g_hi = g.astype(BF16)
    g_lo = (g - g_hi.astype(F32)).astype(BF16)
    tail = _dot(later, g_hi) + _dot(later, g_lo)
    k_d = (gk_ref[...] * jnp.exp(tail)).astype(BF16)
    for h in range(GLA_HEADS):
        p = h // 2
        st_ref[h] = _dot_tn(gv_ref[:, GLA_DV * h:GLA_DV * (h + 1)], k_d[:, LANES * p:LANES * (p + 1)])


def _gla_init(gk_m, gv_m, gf_m):
    return pl.pallas_call(
        _gla_init_kernel,
        out_shape=jax.ShapeDtypeStruct((GLA_HEADS, GLA_DV, LANES), F32),
        name="gla_init",
    )(gk_m, gv_m, gf_m)


def _na_kernel(q_ref, k_ref, v_ref, km_ref, vm_ref, bias_ref, mb_ref, o_ref, *, rows_per_step, n_rows):
    step = pl.program_id(1)
    kh = min(NA_KH_MAX, n_rows)
    masks = _head_masks(GRID_W, BF16)
    sel_lo = lax.broadcasted_iota(jnp.int32, (GRID_W, LANES), 1) < LANES // 2

    def row_body(r, carry):
        i = step * rows_per_step + r
        s = jnp.clip(i - kh // 2, 0, n_rows - kh)
        dr0 = s - i + (NA_KH_MAX - 1)
        qoff = pl.multiple_of(r * GRID_W, GRID_W)
        koff = pl.multiple_of(s * GRID_W, GRID_W)
        n_pairs = NA_HEADS // 2
        pair_lanes = [slice(LANES * p, LANES * (p + 1)) for p in range(n_pairs)]
        scores = []
        for p in range(n_pairs):
            q_pair = q_ref[0, pl.ds(qoff, GRID_W), pair_lanes[p]]
            q2 = jnp.concatenate([q_pair * masks[0], q_pair * masks[1]], axis=0)
            k_win = k_ref[0, pl.ds(koff, kh * GRID_W), pair_lanes[p]]
            scores.append((_dot_nt(q2, k_win), _dot_nt(q2, km_ref[:, pair_lanes[p]])))
        pair_outs = []
        for p in range(n_pairs):
            s_win, s_meta = scores[p]
            s_meta = s_meta + mb_ref[p]
            chunks = [s_win[:, LANES * t:LANES * (t + 1)] + bias_ref[p, dr0 + 2 * t]
                      for t in range(kh * GRID_W // LANES)]
            m_el = functools.reduce(jnp.maximum, chunks)
            m = jnp.maximum(jnp.max(m_el, axis=-1, keepdims=True),
                            jnp.max(s_meta, axis=-1, keepdims=True))
            p_chunks = [jnp.exp(c - m) for c in chunks]
            p_meta = jnp.exp(s_meta - m)
            l = (jnp.sum(functools.reduce(jnp.add, p_chunks), axis=-1, keepdims=True)
                 + jnp.sum(p_meta, axis=-1, keepdims=True))
            p_win = jnp.concatenate(p_chunks, axis=1).astype(BF16)
            v_win = v_ref[0, pl.ds(koff, kh * GRID_W), pair_lanes[p]]
            o2 = (_dot(p_win, v_win) + _dot(p_meta.astype(BF16), vm_ref[:, pair_lanes[p]])) / l
            pair_outs.append(jnp.where(sel_lo, o2[:GRID_W], o2[GRID_W:]).astype(BF16))
        o_ref[0, pl.ds(qoff, GRID_W), :] = jnp.concatenate(pair_outs, axis=1)
        return carry

    lax.fori_loop(0, rows_per_step, row_body, 0)


def _na(q, k, v, k_meta, v_meta, bias_pairs, meta_bias, rows_per_step):
    b, n, w = q.shape
    n_rows = n // GRID_W
    kern = functools.partial(_na_kernel, rows_per_step=rows_per_step, n_rows=n_rows)
    tq = rows_per_step * GRID_W
    return pl.pallas_call(
        kern,
        grid=(b, n_rows // rows_per_step),
        in_specs=[
            pl.BlockSpec((1, tq, w), lambda bi, ri: (bi, ri, 0)),
            pl.BlockSpec((1, n, w), lambda bi, ri: (bi, 0, 0)),
            pl.BlockSpec((1, n, w), lambda bi, ri: (bi, 0, 0)),
            pl.BlockSpec(k_meta.shape, lambda bi, ri: (0, 0)),
            pl.BlockSpec(v_meta.shape, lambda bi, ri: (0, 0)),
            pl.BlockSpec(bias_pairs.shape, lambda bi, ri: (0, 0, 0, 0)),
            pl.BlockSpec(meta_bias.shape, lambda bi, ri: (0, 0, 0)),
        ],
        out_specs=pl.BlockSpec((1, tq, w), lambda bi, ri: (bi, ri, 0)),
        out_shape=jax.ShapeDtypeStruct((b, n, w), BF16),
        compiler_params=pltpu.CompilerParams(dimension_semantics=("arbitrary", "arbitrary"),
                                             vmem_limit_bytes=VMEM_LIMIT),
        name="na",
    )(q, k, v, k_meta, v_meta, bias_pairs, meta_bias)


GLA_GROUP = 4
GLA_FINAL_ROWS = 256


def _gla_kernel(gq_ref, gk_ref, gv_ref, gf_ref, gb_ref, gr_ref, s0_ref, gain_ref, o_ref,
                ofwd_ref, obwd_ref, stf_ref, stb_ref):
    seq = gq_ref.shape[1]
    n_chunks = seq // GLA_CHUNK
    r = lax.broadcasted_iota(jnp.int32, (GLA_CHUNK, GLA_CHUNK), 0)
    c = lax.broadcasted_iota(jnp.int32, (GLA_CHUNK, GLA_CHUNK), 1)
    masks = _head_masks(GLA_CHUNK, F32)
    dirs = ((gf_ref, c <= r, GLA_CHUNK - 1, stf_ref, ofwd_ref),
            (gb_ref, c >= r, 0, stb_ref, obwd_ref))
    tris = [jnp.where(d[1], 1.0, 0.0).astype(BF16) for d in dirs]
    keep2 = [jnp.concatenate([d[1], d[1]], axis=0) for d in dirs]

    stf_ref[...] = s0_ref[...].reshape(stf_ref.shape)
    stb_ref[...] = jnp.zeros_like(stb_ref)

    def step(t, carry):
        chains = []
        for d in range(2):
            for j in range(GLA_GROUP):
                idx = t * GLA_GROUP + j
                n = idx if d == 0 else n_chunks - 1 - idx
                chains.append({"d": d, "rows": pl.ds(pl.multiple_of(n * GLA_CHUNK, GLA_CHUNK), GLA_CHUNK)})
        for ch in chains:
            g = dirs[ch["d"]][0][0, ch["rows"], :]
            g_hi = g.astype(BF16)
            g_lo = (g - g_hi.astype(F32)).astype(BF16)
            ch["b2"] = _dot(tris[ch["d"]], jnp.concatenate([g_hi, g_lo], axis=1))
        for ch in chains:
            last = dirs[ch["d"]][2]
            b = ch["b2"][:, :LANES] + ch["b2"][:, LANES:]
            b_last = b[last:last + 1, :]
            q_e = gq_ref[0, ch["rows"], :] * jnp.exp(b)
            k = gk_ref[0, ch["rows"], :]
            ch["qm2"] = jnp.concatenate([q_e * masks[0], q_e * masks[1]], axis=0).astype(BF16)
            ch["k_e"] = (k * jnp.exp(-b)).astype(BF16)
            ch["k_d"] = (k * jnp.exp(b_last - b)).astype(BF16)
            ch["decay"] = jnp.exp(b_last)
        for ch in chains:
            ch["a2"] = _dot_nt(ch["qm2"], ch["k_e"])
            ch["kv"] = _dot_tn(gv_ref[0, ch["rows"], :], ch["k_d"])
        for d in range(2):
            st_ref = dirs[d][3]
            st = st_ref[...]
            for ch in chains:
                if ch["d"] == d:
                    ch["st_prev"] = st.astype(BF16)
                    st = ch["decay"] * st + ch["kv"]
            st_ref[...] = st
        for ch in chains:
            d = ch["d"]
            a2 = jnp.where(keep2[d], ch["a2"], 0.0).astype(BF16)
            for hh in range(2):
                rows_h = slice(GLA_CHUNK * hh, GLA_CHUNK * (hh + 1))
                cols = slice(GLA_DV * hh, GLA_DV * (hh + 1))
                o = (_dot(a2[rows_h], gv_ref[0, ch["rows"], cols])
                     + _dot_nt(ch["qm2"][rows_h], ch["st_prev"][cols]))
                dirs[d][4][ch["rows"], cols] = o
        return carry

    lax.fori_loop(0, n_chunks // GLA_GROUP, step, 0)

    def finish(i, carry):
        rows = pl.ds(pl.multiple_of(i * GLA_FINAL_ROWS, GLA_FINAL_ROWS), GLA_FINAL_ROWS)
        for hh in range(2):
            cols = slice(GLA_DV * hh, GLA_DV * (hh + 1))
            o = _rms(ofwd_ref[rows, cols] + obwd_ref[rows, cols], gain_ref[...])
            o_ref[0, rows, cols] = (o * gr_ref[0, rows, cols]).astype(BF16)
        return carry

    lax.fori_loop(0, seq // GLA_FINAL_ROWS, finish, 0)


def _gla(gq, gk, gv, gf, gb, gr, s0, gain):
    b, n, _ = gq.shape
    assert (n // GLA_CHUNK) % GLA_GROUP == 0 and n % GLA_FINAL_ROWS == 0
    n_pairs = GLA_HEADS // 2
    key_spec = pl.BlockSpec((1, n, LANES), lambda bi, pi: (bi, 0, pi))
    val_spec = pl.BlockSpec((1, n, 2 * GLA_DV), lambda bi, pi: (bi, 0, pi))
    return pl.pallas_call(
        _gla_kernel,
        grid=(b, n_pairs),
        in_specs=[key_spec, key_spec, val_spec, key_spec, key_spec, val_spec,
                  pl.BlockSpec((2, GLA_DV, LANES), lambda bi, pi: (pi, 0, 0)),
                  pl.BlockSpec((1, GLA_DV), lambda bi, pi: (0, 0))],
        out_specs=val_spec,
        out_shape=jax.ShapeDtypeStruct((b, n, GLA_VAL_WIDTH), BF16),
        scratch_shapes=[pltpu.VMEM((n, 2 * GLA_DV), F32), pltpu.VMEM((n, 2 * GLA_DV), F32),
                        pltpu.VMEM((2 * GLA_DV, LANES), F32), pltpu.VMEM((2 * GLA_DV, LANES), F32)],
        compiler_params=pltpu.CompilerParams(dimension_semantics=("arbitrary", "arbitrary"),
                                             vmem_limit_bytes=VMEM_LIMIT),
        name="gla",
    )(gq, gk, gv, gf, gb, gr, s0, gain)


def _out_ffn_kernel(x_ref, na_ref, gla_ref, wout_ref, gffn_ref, wg_ref, wu_ref, wd_ref, gfin_ref, o_ref):
    mix = _dot(na_ref[...], wout_ref[0:NA_WIDTH, :]) + _dot(gla_ref[...], wout_ref[NA_WIDTH:, :])
    h = x_ref[...] + mix
    hn = _rms(h, gffn_ref[...]).astype(BF16)
    act = (_silu(_dot(hn, wg_ref[...])) * _dot(hn, wu_ref[...])).astype(BF16)
    h = h + _dot(act, wd_ref[...])
    o_ref[...] = _rms(h, gfin_ref[...])


def _out_ffn(x2d, na2d, gla2d, w_out, g_ffn, w_gate, w_up, w_down, g_fin, tm):
    m, d = x2d.shape
    const = lambda shape: pl.BlockSpec(shape, lambda i: (0, 0))
    row = lambda width: pl.BlockSpec((tm, width), lambda i: (i, 0))
    return pl.pallas_call(
        _out_ffn_kernel,
        grid=(m // tm,),
        in_specs=[row(d), row(NA_WIDTH), row(GLA_VAL_WIDTH), const(w_out.shape), const((1, d)),
                  const(w_gate.shape), const(w_up.shape), const(w_down.shape), const((1, d))],
        out_specs=row(d),
        out_shape=jax.ShapeDtypeStruct((m, d), F32),
        compiler_params=pltpu.CompilerParams(dimension_semantics=("arbitrary",),
                                             vmem_limit_bytes=56 * 1024 * 1024),
        name="out_ffn",
    )(x2d, na2d, gla2d, w_out, g_ffn, w_gate, w_up, w_down, g_fin)


def _na_bias_pairs(rpb):
    cols = jnp.arange(GRID_W)
    col_start = jnp.clip(cols - NA_KW // 2, 0, GRID_W - NA_KW)
    valid = (cols[None, :] >= col_start[:, None]) & (cols[None, :] < col_start[:, None] + NA_KW)
    dc = jnp.clip(cols[None, :] - cols[:, None], -(NA_KW - 1), NA_KW - 1) + (NA_KW - 1)
    tiles = jnp.where(valid[None, None], rpb[:, :, dc], NEG_INF).astype(F32)
    two_rows = jnp.concatenate([tiles[:, :-1], tiles[:, 1:]], axis=-1)
    n_dr = two_rows.shape[1]
    stacked = two_rows.reshape(NA_HEADS // 2, 2, n_dr, GRID_W, LANES).transpose(0, 2, 1, 3, 4)
    return stacked.reshape(NA_HEADS // 2, n_dr, 2 * GRID_W, LANES)


def _na_meta_bias_pairs(meta_bias):
    return jnp.repeat(meta_bias.reshape(NA_HEADS // 2, 2, N_META), GRID_W, axis=1)


def kernel(x, meta_tokens, norm_mix_gain, w_in, rpb, meta_bias, w_gate_up_fwd, b_gate_fwd, w_gate_up_bwd, b_gate_bwd, gla_norm_gain, w_out, norm_ffn_gain, w_ffn_gate, w_ffn_up, w_ffn_down, norm_final_gain):
    bsz, seq, d = x.shape
    assert w_in.shape[0] == 1, "single-layer block: meta-token outputs are never consumed"
    assert w_in.shape[2] == IN_WIDTH and seq % GRID_W == 0 and GLA_CHUNK == GRID_W

    w_pad = jnp.concatenate([w_in[0], jnp.zeros((d, IN_WIDTH_PAD - IN_WIDTH), F32)], axis=1).astype(BF16)
    zeros_up = jnp.zeros((GATE_SLAB - 2 * GLA_GATE_RANK, GLA_KEY_WIDTH), F32)
    wupf = jnp.concatenate([w_gate_up_fwd[0], jnp.zeros_like(w_gate_up_bwd[0]), zeros_up], axis=0).astype(BF16)
    wupb = jnp.concatenate([jnp.zeros_like(w_gate_up_fwd[0]), w_gate_up_bwd[0], zeros_up], axis=0).astype(BF16)
    bgf = b_gate_fwd[0][None, :]
    bgb = b_gate_bwd[0][None, :]
    gain_mix = norm_mix_gain[0][None, :]

    x2d = x.reshape(bsz * seq, d)
    proj_args = (gain_mix, w_pad, wupf, bgf, wupb, bgb)
    naq, nak, nav, gq, gk, gv, gr, gf, gb = _in_proj(x2d, *proj_args, tm=512)
    _, km, vm, _, gk_m, gv_m, _, gf_m, _ = _in_proj(meta_tokens, *proj_args, tm=N_META)

    s0 = _gla_init(gk_m, gv_m, gf_m)

    per_batch = lambda a: a.reshape(bsz, seq, a.shape[-1])
    na_out = _na(per_batch(naq), per_batch(nak), per_batch(nav), km, vm,
                 _na_bias_pairs(rpb[0]), _na_meta_bias_pairs(meta_bias[0]), rows_per_step=8)
    gla_out = _gla(per_batch(gq), per_batch(gk), per_batch(gv), per_batch(gf), per_batch(gb),
                   per_batch(gr), s0, gla_norm_gain[0][None, :])

    out = _out_ffn(x2d, na_out.reshape(bsz * seq, NA_WIDTH), gla_out.reshape(bsz * seq, GLA_VAL_WIDTH),
                   w_out[0].astype(BF16), norm_ffn_gain[0][None, :], w_ffn_gate[0].astype(BF16),
                   w_ffn_up[0].astype(BF16), w_ffn_down[0].astype(BF16), norm_final_gain[None, :], tm=256)
    return out.reshape(bsz, seq, d)
```

```python
import functools

import jax
import jax.numpy as jnp
from jax import lax
from jax.experimental import pallas as pl
from jax.experimental.pallas import tpu as pltpu

F32 = jnp.float32
BF16 = jnp.bfloat16

N_META = 16
GRID_W = 64
NA_HEADS = 8
NA_HEAD_DIM = 64
NA_WIDTH = NA_HEADS * NA_HEAD_DIM
NA_KH_MAX = 8
NA_KW = 16
GLA_HEADS = 4
GLA_DK = 64
GLA_DV = 128
GLA_KEY_WIDTH = GLA_HEADS * GLA_DK
GLA_VAL_WIDTH = GLA_HEADS * GLA_DV
GLA_GATE_RANK = 16
GLA_GATE_TAU = 16.0
GLA_CHUNK = 64
RMS_EPS = 1e-6
NEG_INF = -1e30

LANES = 128
GATE_SLAB = LANES
VMEM_LIMIT = 48 * 1024 * 1024

_OFF_NA_Q = 0
_OFF_NA_K = _OFF_NA_Q + NA_WIDTH
_OFF_NA_V = _OFF_NA_K + NA_WIDTH
_OFF_G_Q = _OFF_NA_V + NA_WIDTH
_OFF_G_K = _OFF_G_Q + GLA_KEY_WIDTH
_OFF_G_V = _OFF_G_K + GLA_KEY_WIDTH
_OFF_G_R = _OFF_G_V + GLA_VAL_WIDTH
_OFF_GATE = _OFF_G_R + GLA_VAL_WIDTH
IN_WIDTH = _OFF_GATE + 2 * GLA_GATE_RANK
IN_WIDTH_PAD = _OFF_GATE + GATE_SLAB


def _rms(x, gain):
    ms = jnp.mean(x * x, axis=-1, keepdims=True)
    return x * lax.rsqrt(ms + RMS_EPS) * gain


def _silu(x):
    return x * (1.0 / (1.0 + jnp.exp(-x)))


def _log_sigmoid(x):
    return jnp.minimum(x, 0.0) - jnp.log1p(jnp.exp(-jnp.abs(x)))


def _dot(a, b):
    return jnp.dot(a, b, preferred_element_type=F32)


def _dot_nt(a, b):
    return lax.dot_general(a, b, (((1,), (1,)), ((), ())), preferred_element_type=F32)


def _dot_tn(a, b):
    return lax.dot_general(a, b, (((0,), (0,)), ((), ())), preferred_element_type=F32)


def _head_masks(rows, dtype):
    lane = lax.broadcasted_iota(jnp.int32, (rows, LANES), 1)
    lo = jnp.where(lane < LANES // 2, 1.0, 0.0).astype(dtype)
    hi = jnp.where(lane >= LANES // 2, 1.0, 0.0).astype(dtype)
    return lo, hi


def _in_proj_kernel(x_ref, gain_ref, w_ref, wupf_ref, bf_ref, wupb_ref, bb_ref,
                    naq_ref, nak_ref, nav_ref, gq_ref, gk_ref, gv_ref, gr_ref, gf_ref, gb_ref):
    xn = _rms(x_ref[...], gain_ref[...]).astype(BF16)

    def proj(lo, width):
        return _dot(xn, w_ref[:, lo:lo + width])

    naq_ref[...] = (proj(_OFF_NA_Q, NA_WIDTH) * (NA_HEAD_DIM ** -0.5)).astype(BF16)
    nak_ref[...] = proj(_OFF_NA_K, NA_WIDTH).astype(BF16)
    nav_ref[...] = proj(_OFF_NA_V, NA_WIDTH).astype(BF16)
    gq_ref[...] = proj(_OFF_G_Q, GLA_KEY_WIDTH) * (GLA_DK ** -0.5)
    gk_ref[...] = proj(_OFF_G_K, GLA_KEY_WIDTH)
    gv_ref[...] = proj(_OFF_G_V, GLA_VAL_WIDTH).astype(BF16)
    gr_ref[...] = _silu(proj(_OFF_G_R, GLA_VAL_WIDTH))
    low = proj(_OFF_GATE, GATE_SLAB).astype(BF16)
    gf_ref[...] = _log_sigmoid(_dot(low, wupf_ref[...]) + bf_ref[...]) * (1.0 / GLA_GATE_TAU)
    gb_ref[...] = _log_sigmoid(_dot(low, wupb_ref[...]) + bb_ref[...]) * (1.0 / GLA_GATE_TAU)


def _in_proj(x2d, gain, w_pad, wupf, bgf, wupb, bgb, tm):
    m, d = x2d.shape
    const = lambda shape: pl.BlockSpec(shape, lambda i: (0, 0))
    row = lambda width: pl.BlockSpec((tm, width), lambda i: (i, 0))
    widths = (NA_WIDTH, NA_WIDTH, NA_WIDTH, GLA_KEY_WIDTH, GLA_KEY_WIDTH,
              GLA_VAL_WIDTH, GLA_VAL_WIDTH, GLA_KEY_WIDTH, GLA_KEY_WIDTH)
    dtypes = (BF16, BF16, BF16, F32, F32, BF16, F32, F32, F32)
    return pl.pallas_call(
        _in_proj_kernel,
        grid=(m // tm,),
        in_specs=[row(d), const((1, d)), const(w_pad.shape), const(wupf.shape), const(bgf.shape),
                  const(wupb.shape), const(bgb.shape)],
        out_specs=[row(w) for w in widths],
        out_shape=[jax.ShapeDtypeStruct((m, w), dt) for w, dt in zip(widths, dtypes)],
        compiler_params=pltpu.CompilerParams(dimension_semantics=("arbitrary",),
                                             vmem_limit_bytes=VMEM_LIMIT),
        name="in_proj",
    )(x2d, gain, w_pad, wupf, bgf, wupb, bgb)


def _gla_init_kernel(gk_ref, gv_ref, gf_ref, st_ref):
    n = gk_ref.shape[0]
    r = lax.broadcasted_iota(jnp.int32, (n, n), 0)
    c = lax.broadcasted_iota(jnp.int32, (n, n), 1)
    later = jnp.where(c > r, 1.0, 0.0).astype(BF16)
    g = gf_ref[...]
    g_hi = g.astype(BF16)
    g_lo = (g - g_hi.astype(F32)).astype(BF16)
    tail = _dot(later, g_hi) + _dot(later, g_lo)
    k_d = (gk_ref[...] * jnp.exp(tail)).astype(BF16)
    for h in range(GLA_HEADS):
        p = h // 2
        st_ref[h] = _dot_tn(gv_ref[:, GLA_DV * h:GLA_DV * (h + 1)], k_d[:, LANES * p:LANES * (p + 1)])


def _gla_init(gk_m, gv_m, gf_m):
    return pl.pallas_call(
        _gla_init_kernel,
        out_shape=jax.ShapeDtypeStruct((GLA_HEADS, GLA_DV, LANES), F32),
        name="gla_init",
    )(gk_m, gv_m, gf_m)


def _na_kernel(q_ref, k_ref, v_ref, km_ref, vm_ref, bias_ref, mb_ref, o_ref, *, rows_per_step, n_rows):
    step = pl.program_id(1)
    kh = min(NA_KH_MAX, n_rows)
    masks = _head_masks(GRID_W, BF16)
    sel_lo = lax.broadcasted_iota(jnp.int32, (GRID_W, LANES), 1) < LANES // 2

    def row_body(r, carry):
        i = step * rows_per_step + r
        s = jnp.clip(i - kh // 2, 0, n_rows - kh)
        dr0 = s - i + (NA_KH_MAX - 1)
        qoff = pl.multiple_of(r * GRID_W, GRID_W)
        koff = pl.multiple_of(s * GRID_W, GRID_W)
        n_pairs = NA_HEADS // 2
        pair_lanes = [slice(LANES * p, LANES * (p + 1)) for p in range(n_pairs)]
        scores = []
        for p in range(n_pairs):
            q_pair = q_ref[0, pl.ds(qoff, GRID_W), pair_lanes[p]]
            q2 = jnp.concatenate([q_pair * masks[0], q_pair * masks[1]], axis=0)
            k_win = k_ref[0, pl.ds(koff, kh * GRID_W), pair_lanes[p]]
            scores.append((_dot_nt(q2, k_win), _dot_nt(q2, km_ref[:, pair_lanes[p]])))
        biased = []
        for p in range(n_pairs):
            s_win, s_meta = scores[p]
            s_meta = s_meta + mb_ref[p]
            chunks = [s_win[:, LANES * t:LANES * (t + 1)] + bias_ref[p, dr0 + 2 * t]
                      for t in range(kh * GRID_W // LANES)]
            m_el = functools.reduce(jnp.maximum, chunks)
            m = jnp.maximum(jnp.max(m_el, axis=-1, keepdims=True),
                            jnp.max(s_meta, axis=-1, keepdims=True))
            biased.append((chunks, s_meta, m))
        probs = []
        for p in range(n_pairs):
            chunks, s_meta, m = biased[p]
            p_chunks = [jnp.exp(c - m) for c in chunks]
            p_meta = jnp.exp(s_meta - m)
            l = (jnp.sum(functools.reduce(jnp.add, p_chunks), axis=-1, keepdims=True)
                 + jnp.sum(p_meta, axis=-1, keepdims=True))
            probs.append((jnp.concatenate(p_chunks, axis=1).astype(BF16), p_meta.astype(BF16), l))
        pair_outs = []
        for p in range(n_pairs):
            p_win, p_meta, l = probs[p]
            v_win = v_ref[0, pl.ds(koff, kh * GRID_W), pair_lanes[p]]
            o2 = (_dot(p_win, v_win) + _dot(p_meta, vm_ref[:, pair_lanes[p]])) / l
            pair_outs.append(jnp.where(sel_lo, o2[:GRID_W], o2[GRID_W:]).astype(BF16))
        o_ref[0, pl.ds(qoff, GRID_W), :] = jnp.concatenate(pair_outs, axis=1)
        return carry

    lax.fori_loop(0, rows_per_step, row_body, 0)


def _na(q, k, v, k_meta, v_meta, bias_pairs, meta_bias, rows_per_step):
    b, n, w = q.shape
    n_rows = n // GRID_W
    kern = functools.partial(_na_kernel, rows_per_step=rows_per_step, n_rows=n_rows)
    tq = rows_per_step * GRID_W
    return pl.pallas_call(
        kern,
        grid=(b, n_rows // rows_per_step),
        in_specs=[
            pl.BlockSpec((1, tq, w), lambda bi, ri: (bi, ri, 0)),
            pl.BlockSpec((1, n, w), lambda bi, ri: (bi, 0, 0)),
            pl.BlockSpec((1, n, w), lambda bi, ri: (bi, 0, 0)),
            pl.BlockSpec(k_meta.shape, lambda bi, ri: (0, 0)),
            pl.BlockSpec(v_meta.shape, lambda bi, ri: (0, 0)),
            pl.BlockSpec(bias_pairs.shape, lambda bi, ri: (0, 0, 0, 0)),
            pl.BlockSpec(meta_bias.shape, lambda bi, ri: (0, 0, 0)),
        ],
        out_specs=pl.BlockSpec((1, tq, w), lambda bi, ri: (bi, ri, 0)),
        out_shape=jax.ShapeDtypeStruct((b, n, w), BF16),
        compiler_params=pltpu.CompilerParams(dimension_semantics=("arbitrary", "arbitrary"),
                                             vmem_limit_bytes=VMEM_LIMIT),
        name="na",
    )(q, k, v, k_meta, v_meta, bias_pairs, meta_bias)


GLA_GROUP = 4
GLA_FINAL_ROWS = 256


def _gla_kernel(gq_ref, gk_ref, gv_ref, gf_ref, gb_ref, gr_ref, s0_ref, gain_ref, o_ref,
                ofwd_ref, obwd_ref, stf_ref, stb_ref):
    seq = gq_ref.shape[1]
    n_chunks = seq // GLA_CHUNK
    r = lax.broadcasted_iota(jnp.int32, (GLA_CHUNK, GLA_CHUNK), 0)
    c = lax.broadcasted_iota(jnp.int32, (GLA_CHUNK, GLA_CHUNK), 1)
    masks = _head_masks(GLA_CHUNK, F32)
    dirs = ((gf_ref, c <= r, GLA_CHUNK - 1, stf_ref, ofwd_ref),
            (gb_ref, c >= r, 0, stb_ref, obwd_ref))
    tris = [jnp.where(d[1], 1.0, 0.0).astype(BF16) for d in dirs]
    keep2 = [jnp.concatenate([d[1], d[1]], axis=0) for d in dirs]

    stf_ref[...] = s0_ref[...].reshape(stf_ref.shape)
    stb_ref[...] = jnp.zeros_like(stb_ref)

    def step(t, carry):
        chains = []
        for d in range(2):
            for j in range(GLA_GROUP):
                idx = t * GLA_GROUP + j
                n = idx if d == 0 else n_chunks - 1 - idx
                chains.append({"d": d, "rows": pl.ds(pl.multiple_of(n * GLA_CHUNK, GLA_CHUNK), GLA_CHUNK)})
        for ch in chains:
            g = dirs[ch["d"]][0][0, ch["rows"], :]
            g_hi = g.astype(BF16)
            g_lo = (g - g_hi.astype(F32)).astype(BF16)
            ch["b2"] = _dot(tris[ch["d"]], jnp.concatenate([g_hi, g_lo], axis=1))
        for ch in chains:
            last = dirs[ch["d"]][2]
            b = ch["b2"][:, :LANES] + ch["b2"][:, LANES:]
            b_last = b[last:last + 1, :]
            q_e = gq_ref[0, ch["rows"], :] * jnp.exp(b)
            k = gk_ref[0, ch["rows"], :]
            ch["qm2"] = jnp.concatenate([q_e * masks[0], q_e * masks[1]], axis=0).astype(BF16)
            ch["k_e"] = (k * jnp.exp(-b)).astype(BF16)
            ch["k_d"] = (k * jnp.exp(b_last - b)).astype(BF16)
            ch["decay"] = jnp.exp(b_last)
        for ch in chains:
            ch["a2"] = _dot_nt(ch["qm2"], ch["k_e"])
            ch["kv"] = _dot_tn(gv_ref[0, ch["rows"], :], ch["k_d"])
        for d in range(2):
            st_ref = dirs[d][3]
            st = st_ref[...]
            for ch in chains:
                if ch["d"] == d:
                    ch["st_prev"] = st.astype(BF16)
                    st = ch["decay"] * st + ch["kv"]
            st_ref[...] = st
        for ch in chains:
            d = ch["d"]
            a2 = jnp.where(keep2[d], ch["a2"], 0.0).astype(BF16)
            for hh in range(2):
                rows_h = slice(GLA_CHUNK * hh, GLA_CHUNK * (hh + 1))
                cols = slice(GLA_DV * hh, GLA_DV * (hh + 1))
                o = (_dot(a2[rows_h], gv_ref[0, ch["rows"], cols])
                     + _dot_nt(ch["qm2"][rows_h], ch["st_prev"][cols]))
                dirs[d][4][ch["rows"], cols] = o
        return carry

    lax.fori_loop(0, n_chunks // GLA_GROUP, step, 0)

    def finish(i, carry):
        rows = pl.ds(pl.multiple_of(i * GLA_FINAL_ROWS, GLA_FINAL_ROWS), GLA_FINAL_ROWS)
        for hh in range(2):
            cols = slice(GLA_DV * hh, GLA_DV * (hh + 1))
            o = _rms(ofwd_ref[rows, cols] + obwd_ref[rows, cols], gain_ref[...])
            o_ref[0, rows, cols] = (o * gr_ref[0, rows, cols]).astype(BF16)
        return carry

    lax.fori_loop(0, seq // GLA_FINAL_ROWS, finish, 0)


def _gla(gq, gk, gv, gf, gb, gr, s0, gain):
    b, n, _ = gq.shape
    assert (n // GLA_CHUNK) % GLA_GROUP == 0 and n % GLA_FINAL_ROWS == 0
    n_pairs = GLA_HEADS // 2
    key_spec = pl.BlockSpec((1, n, LANES), lambda bi, pi: (bi, 0, pi))
    val_spec = pl.BlockSpec((1, n, 2 * GLA_DV), lambda bi, pi: (bi, 0, pi))
    return pl.pallas_call(
        _gla_kernel,
        grid=(b, n_pairs),
        in_specs=[key_spec, key_spec, val_spec, key_spec, key_spec, val_spec,
                  pl.BlockSpec((2, GLA_DV, LANES), lambda bi, pi: (pi, 0, 0)),
                  pl.BlockSpec((1, GLA_DV), lambda bi, pi: (0, 0))],
        out_specs=val_spec,
        out_shape=jax.ShapeDtypeStruct((b, n, GLA_VAL_WIDTH), BF16),
        scratch_shapes=[pltpu.VMEM((n, 2 * GLA_DV), F32), pltpu.VMEM((n, 2 * GLA_DV), F32),
                        pltpu.VMEM((2 * GLA_DV, LANES), F32), pltpu.VMEM((2 * GLA_DV, LANES), F32)],
        compiler_params=pltpu.CompilerParams(dimension_semantics=("arbitrary", "arbitrary"),
                                             vmem_limit_bytes=VMEM_LIMIT),
        name="gla",
    )(gq, gk, gv, gf, gb, gr, s0, gain)


def _out_ffn_kernel(x_ref, na_ref, gla_ref, wout_ref, gffn_ref, wg_ref, wu_ref, wd_ref, gfin_ref, o_ref):
    mix = _dot(na_ref[...], wout_ref[0:NA_WIDTH, :]) + _dot(gla_ref[...], wout_ref[NA_WIDTH:, :])
    h = x_ref[...] + mix
    hn = _rms(h, gffn_ref[...]).astype(BF16)
    act = (_silu(_dot(hn, wg_ref[...])) * _dot(hn, wu_ref[...])).astype(BF16)
    h = h + _dot(act, wd_ref[...])
    o_ref[...] = _rms(h, gfin_ref[...])


def _out_ffn(x2d, na2d, gla2d, w_out, g_ffn, w_gate, w_up, w_down, g_fin, tm):
    m, d = x2d.shape
    const = lambda shape: pl.BlockSpec(shape, lambda i: (0, 0))
    row = lambda width: pl.BlockSpec((tm, width), lambda i: (i, 0))
    return pl.pallas_call(
        _out_ffn_kernel,
        grid=(m // tm,),
        in_specs=[row(d), row(NA_WIDTH), row(GLA_VAL_WIDTH), const(w_out.shape), const((1, d)),
                  const(w_gate.shape), const(w_up.shape), const(w_down.shape), const((1, d))],
        out_specs=row(d),
        out_shape=jax.ShapeDtypeStruct((m, d), F32),
        compiler_params=pltpu.CompilerParams(dimension_semantics=("arbitrary",),
                                             vmem_limit_bytes=56 * 1024 * 1024),
        name="out_ffn",
    )(x2d, na2d, gla2d, w_out, g_ffn, w_gate, w_up, w_down, g_fin)


def _na_bias_pairs(rpb):
    cols = jnp.arange(GRID_W)
    col_start = jnp.clip(cols - NA_KW // 2, 0, GRID_W - NA_KW)
    valid = (cols[None, :] >= col_start[:, None]) & (cols[None, :] < col_start[:, None] + NA_KW)
    dc = jnp.clip(cols[None, :] - cols[:, None], -(NA_KW - 1), NA_KW - 1) + (NA_KW - 1)
    tiles = jnp.where(valid[None, None], rpb[:, :, dc], NEG_INF).astype(F32)
    two_rows = jnp.concatenate([tiles[:, :-1], tiles[:, 1:]], axis=-1)
    n_dr = two_rows.shape[1]
    stacked = two_rows.reshape(NA_HEADS // 2, 2, n_dr, GRID_W, LANES).transpose(0, 2, 1, 3, 4)
    return stacked.reshape(NA_HEADS // 2, n_dr, 2 * GRID_W, LANES)


def _na_meta_bias_pairs(meta_bias):
    return jnp.repeat(meta_bias.reshape(NA_HEADS // 2, 2, N_META), GRID_W, axis=1)


def kernel(x, meta_tokens, norm_mix_gain, w_in, rpb, meta_bias, w_gate_up_fwd, b_gate_fwd, w_gate_up_bwd, b_gate_bwd, gla_norm_gain, w_out, norm_ffn_gain, w_ffn_gate, w_ffn_up, w_ffn_down, norm_final_gain):
    bsz, seq, d = x.shape
    assert w_in.shape[0] == 1, "single-layer block: meta-token outputs are never consumed"
    assert w_in.shape[2] == IN_WIDTH and seq % GRID_W == 0 and GLA_CHUNK == GRID_W

    w_pad = jnp.concatenate([w_in[0], jnp.zeros((d, IN_WIDTH_PAD - IN_WIDTH), F32)], axis=1).astype(BF16)
    zeros_up = jnp.zeros((GATE_SLAB - 2 * GLA_GATE_RANK, GLA_KEY_WIDTH), F32)
    wupf = jnp.concatenate([w_gate_up_fwd[0], jnp.zeros_like(w_gate_up_bwd[0]), zeros_up], axis=0).astype(BF16)
    wupb = jnp.concatenate([jnp.zeros_like(w_gate_up_fwd[0]), w_gate_up_bwd[0], zeros_up], axis=0).astype(BF16)
    bgf = b_gate_fwd[0][None, :]
    bgb = b_gate_bwd[0][None, :]
    gain_mix = norm_mix_gain[0][None, :]

    x2d = x.reshape(bsz * seq, d)
    proj_args = (gain_mix, w_pad, wupf, bgf, wupb, bgb)
    naq, nak, nav, gq, gk, gv, gr, gf, gb = _in_proj(x2d, *proj_args, tm=512)
    _, km, vm, _, gk_m, gv_m, _, gf_m, _ = _in_proj(meta_tokens, *proj_args, tm=N_META)

    s0 = _gla_init(gk_m, gv_m, gf_m)

    per_batch = lambda a: a.reshape(bsz, seq, a.shape[-1])
    na_out = _na(per_batch(naq), per_batch(nak), per_batch(nav), km, vm,
                 _na_bias_pairs(rpb[0]), _na_meta_bias_pairs(meta_bias[0]), rows_per_step=8)
    gla_out = _gla(per_batch(gq), per_batch(gk), per_batch(gv), per_batch(gf), per_batch(gb),
                   per_batch(gr), s0, gla_norm_gain[0][None, :])

    out = _out_ffn(x2d, na_out.reshape(bsz * seq, NA_WIDTH), gla_out.reshape(bsz * seq, GLA_VAL_WIDTH),
                   w_out[0].astype(BF16), norm_ffn_gain[0][None, :], w_ffn_gate[0].astype(BF16),
                   w_ffn_up[0].astype(BF16), w_ffn_down[0].astype(BF16), norm_final_gain[None, :], tm=256)
    return out.reshape(bsz, seq, d)
```

```python
import functools

import jax
import jax.numpy as jnp
import numpy as np
from jax import lax
from jax.experimental import pallas as pl
from jax.experimental.pallas import tpu as pltpu

F32 = jnp.float32
BF16 = jnp.bfloat16

N_META = 16
GRID_W = 64
NA_HEADS = 8
NA_HEAD_DIM = 64
NA_WIDTH = NA_HEADS * NA_HEAD_DIM
NA_KH_MAX = 8
NA_KW = 16
GLA_HEADS = 4
GLA_DK = 64
GLA_DV = 128
GLA_KEY_WIDTH = GLA_HEADS * GLA_DK
GLA_VAL_WIDTH = GLA_HEADS * GLA_DV
GLA_GATE_RANK = 16
GLA_GATE_TAU = 16.0
GLA_CHUNK = 64
RMS_EPS = 1e-6
NEG_INF = -1e30
LOG2_E = 1.4426950408889634
NA_Q_SCALE = NA_HEAD_DIM ** -0.5 * LOG2_E

LANES = 128
GATE_SLAB = LANES
VMEM_LIMIT = 48 * 1024 * 1024

_OFF_NA_Q = 0
_OFF_NA_K = _OFF_NA_Q + NA_WIDTH
_OFF_NA_V = _OFF_NA_K + NA_WIDTH
_OFF_G_Q = _OFF_NA_V + NA_WIDTH
_OFF_G_K = _OFF_G_Q + GLA_KEY_WIDTH
_OFF_G_V = _OFF_G_K + GLA_KEY_WIDTH
_OFF_G_R = _OFF_G_V + GLA_VAL_WIDTH
_OFF_GATE = _OFF_G_R + GLA_VAL_WIDTH
IN_WIDTH = _OFF_GATE + 2 * GLA_GATE_RANK
IN_WIDTH_PAD = _OFF_GATE + GATE_SLAB


def _rms(x, gain):
    ms = jnp.mean(x * x, axis=-1, keepdims=True)
    return x * lax.rsqrt(ms + RMS_EPS) * gain


def _silu(x):
    return x * (1.0 / (1.0 + jnp.exp(-x)))


def _log_sigmoid(x):
    return jnp.minimum(x, 0.0) - jnp.log1p(jnp.exp(-jnp.abs(x)))


def _dot(a, b):
    return jnp.dot(a, b, preferred_element_type=F32)


def _dot_nt(a, b):
    return lax.dot_general(a, b, (((1,), (1,)), ((), ())), preferred_element_type=F32)


def _dot_tn(a, b):
    return lax.dot_general(a, b, (((0,), (0,)), ((), ())), preferred_element_type=F32)


def _head_masks(rows, dtype):
    lane = lax.broadcasted_iota(jnp.int32, (rows, LANES), 1)
    lo = jnp.where(lane < LANES // 2, 1.0, 0.0).astype(dtype)
    hi = jnp.where(lane >= LANES // 2, 1.0, 0.0).astype(dtype)
    return lo, hi


def _in_proj_kernel(x_ref, gain_ref, w_ref, wupf_ref, bf_ref, wupb_ref, bb_ref,
                    naq_ref, nak_ref, nav_ref, gq_ref, gk_ref, gv_ref, gr_ref, gf_ref, gb_ref):
    xn = _rms(x_ref[...], gain_ref[...]).astype(BF16)

    def proj(lo, width):
        return _dot(xn, w_ref[:, lo:lo + width])

    naq_ref[...] = (proj(_OFF_NA_Q, NA_WIDTH) * NA_Q_SCALE).astype(BF16)
    nak_ref[...] = proj(_OFF_NA_K, NA_WIDTH).astype(BF16)
    nav_ref[...] = proj(_OFF_NA_V, NA_WIDTH).astype(BF16)
    gq_ref[...] = proj(_OFF_G_Q, GLA_KEY_WIDTH) * (GLA_DK ** -0.5)
    gk_ref[...] = proj(_OFF_G_K, GLA_KEY_WIDTH)
    gv_ref[...] = proj(_OFF_G_V, GLA_VAL_WIDTH).astype(BF16)
    gr_ref[...] = _silu(proj(_OFF_G_R, GLA_VAL_WIDTH))
    low = proj(_OFF_GATE, GATE_SLAB).astype(BF16)
    gf_ref[...] = _log_sigmoid(_dot(low, wupf_ref[...]) + bf_ref[...]) * (1.0 / GLA_GATE_TAU)
    gb_ref[...] = _log_sigmoid(_dot(low, wupb_ref[...]) + bb_ref[...]) * (1.0 / GLA_GATE_TAU)


def _in_proj(x2d, gain, w_pad, wupf, bgf, wupb, bgb, tm):
    m, d = x2d.shape
    const = lambda shape: pl.BlockSpec(shape, lambda i: (0, 0))
    row = lambda width: pl.BlockSpec((tm, width), lambda i: (i, 0))
    widths = (NA_WIDTH, NA_WIDTH, NA_WIDTH, GLA_KEY_WIDTH, GLA_KEY_WIDTH,
              GLA_VAL_WIDTH, GLA_VAL_WIDTH, GLA_KEY_WIDTH, GLA_KEY_WIDTH)
    dtypes = (BF16, BF16, BF16, F32, F32, BF16, F32, F32, F32)
    return pl.pallas_call(
        _in_proj_kernel,
        grid=(m // tm,),
        in_specs=[row(d), const((1, d)), const(w_pad.shape), const(wupf.shape), const(bgf.shape),
                  const(wupb.shape), const(bgb.shape)],
        out_specs=[row(w) for w in widths],
        out_shape=[jax.ShapeDtypeStruct((m, w), dt) for w, dt in zip(widths, dtypes)],
        compiler_params=pltpu.CompilerParams(dimension_semantics=("arbitrary",),
                                             vmem_limit_bytes=VMEM_LIMIT),
        name="in_proj",
    )(x2d, gain, w_pad, wupf, bgf, wupb, bgb)


def _gla_init_kernel(gk_ref, gv_ref, gf_ref, st_ref):
    n = gk_ref.shape[0]
    r = lax.broadcasted_iota(jnp.int32, (n, n), 0)
    c = lax.broadcasted_iota(jnp.int32, (n, n), 1)
    later = jnp.where(c > r, 1.0, 0.0).astype(BF16)
    g = gf_ref[...]
    g_hi = g.astype(BF16)
    g_lo = (g - g_hi.astype(F32)).astype(BF16)
    tail = _dot(later, g_hi) + _dot(later, g_lo)
    k_d = (gk_ref[...] * jnp.exp(tail)).astype(BF16)
    for h in range(GLA_HEADS):
        p = h // 2
        st_ref[h] = _dot_tn(gv_ref[:, GLA_DV * h:GLA_DV * (h + 1)], k_d[:, LANES * p:LANES * (p + 1)])


def _gla_init(gk_m, gv_m, gf_m):
    return pl.pallas_call(
        _gla_init_kernel,
        out_shape=jax.ShapeDtypeStruct((GLA_HEADS, GLA_DV, LANES), F32),
        name="gla_init",
    )(gk_m, gv_m, gf_m)


def _na_kernel(q_ref, k_ref, v_ref, km_ref, vm_ref, bias_ref, mb_ref, o_ref, sw_ref, sm_ref, *, n_rows):
    kh = min(NA_KH_MAX, n_rows)
    n_pairs = NA_HEADS // 2
    pair_lanes = [slice(LANES * p, LANES * (p + 1)) for p in range(n_pairs)]
    masks = _head_masks(GRID_W, BF16)
    sel_lo = lax.broadcasted_iota(jnp.int32, (GRID_W, LANES), 1) < LANES // 2

    def window_start(i):
        return jnp.clip(i - kh // 2, 0, n_rows - kh)

    def row_scores(i, slot):
        qoff = pl.multiple_of(i * GRID_W, GRID_W)
        koff = pl.multiple_of(window_start(i) * GRID_W, GRID_W)
        for p in range(n_pairs):
            q_pair = q_ref[0, pl.ds(qoff, GRID_W), pair_lanes[p]]
            q2 = jnp.concatenate([q_pair * masks[0], q_pair * masks[1]], axis=0)
            k_win = k_ref[0, pl.ds(koff, kh * GRID_W), pair_lanes[p]]
            sw_ref[slot, p] = _dot_nt(q2, k_win)
            sm_ref[slot, p] = _dot_nt(q2, km_ref[:, pair_lanes[p]])

    def row_finish(i, slot):
        s = window_start(i)
        dr0 = s - i + (NA_KH_MAX - 1)
        qoff = pl.multiple_of(i * GRID_W, GRID_W)
        koff = pl.multiple_of(s * GRID_W, GRID_W)
        biased = []
        for p in range(n_pairs):
            s_meta = sm_ref[slot, p] + mb_ref[p]
            chunks = [sw_ref[slot, p, :, LANES * t:LANES * (t + 1)] + bias_ref[p, dr0 + 2 * t]
                      for t in range(kh * GRID_W // LANES)]
            m_el = functools.reduce(jnp.maximum, chunks)
            m = jnp.maximum(jnp.max(m_el, axis=-1, keepdims=True),
                            jnp.max(s_meta, axis=-1, keepdims=True))
            biased.append((chunks, s_meta, m))
        probs = []
        for p in range(n_pairs):
            chunks, s_meta, m = biased[p]
            p_chunks = [jnp.exp2(c - m) for c in chunks]
            p_meta = jnp.exp2(s_meta - m)
            l = (jnp.sum(functools.reduce(jnp.add, p_chunks), axis=-1, keepdims=True)
                 + jnp.sum(p_meta, axis=-1, keepdims=True))
            probs.append((jnp.concatenate(p_chunks, axis=1).astype(BF16), p_meta.astype(BF16), l))
        pair_outs = []
        for p in range(n_pairs):
            p_win, p_meta, l = probs[p]
            v_win = v_ref[0, pl.ds(koff, kh * GRID_W), pair_lanes[p]]
            o2 = (_dot(p_win, v_win) + _dot(p_meta, vm_ref[:, pair_lanes[p]])) / l
            pair_outs.append(jnp.where(sel_lo, o2[:GRID_W], o2[GRID_W:]).astype(BF16))
        o_ref[0, pl.ds(qoff, GRID_W), :] = jnp.concatenate(pair_outs, axis=1)

    def two_rows(j, carry):
        a = 2 * j
        row_scores(a + 1, 1)
        row_finish(a, 0)
        row_scores(jnp.minimum(a + 2, n_rows - 1), 0)
        row_finish(a + 1, 1)
        return carry

    row_scores(0, 0)
    lax.fori_loop(0, n_rows // 2, two_rows, 0)


def _na(q, k, v, k_meta, v_meta, bias_pairs, meta_bias):
    b, n, w = q.shape
    n_rows = n // GRID_W
    assert n_rows % 2 == 0
    kh = min(NA_KH_MAX, n_rows)
    kern = functools.partial(_na_kernel, n_rows=n_rows)
    batch_spec = pl.BlockSpec((1, n, w), lambda bi: (bi, 0, 0))
    whole = lambda a: pl.BlockSpec(a.shape, lambda bi: (0,) * a.ndim)
    return pl.pallas_call(
        kern,
        grid=(b,),
        in_specs=[batch_spec, batch_spec, batch_spec, whole(k_meta), whole(v_meta), whole(bias_pairs),
                  whole(meta_bias)],
        out_specs=batch_spec,
        out_shape=jax.ShapeDtypeStruct((b, n, w), BF16),
        scratch_shapes=[pltpu.VMEM((2, NA_HEADS // 2, 2 * GRID_W, kh * GRID_W), F32),
                        pltpu.VMEM((2, NA_HEADS // 2, 2 * GRID_W, N_META), F32)],
        compiler_params=pltpu.CompilerParams(dimension_semantics=("arbitrary",),
                                             vmem_limit_bytes=VMEM_LIMIT),
        name="na",
    )(q, k, v, k_meta, v_meta, bias_pairs, meta_bias)


GLA_GROUP = 4
GLA_FINAL_ROWS = 256


def _gla_kernel(gq_ref, gk_ref, gv_ref, gf_ref, gb_ref, gr_ref, s0_ref, gain_ref, o_ref,
                ofwd_ref, obwd_ref, stf_ref, stb_ref):
    seq = gq_ref.shape[1]
    n_chunks = seq // GLA_CHUNK
    r = lax.broadcasted_iota(jnp.int32, (GLA_CHUNK, GLA_CHUNK), 0)
    c = lax.broadcasted_iota(jnp.int32, (GLA_CHUNK, GLA_CHUNK), 1)
    masks = _head_masks(GLA_CHUNK, F32)
    dirs = ((gf_ref, c <= r, GLA_CHUNK - 1, stf_ref, ofwd_ref),
            (gb_ref, c >= r, 0, stb_ref, obwd_ref))
    tris = [jnp.where(d[1], 1.0, 0.0).astype(BF16) for d in dirs]
    keep2 = [jnp.concatenate([d[1], d[1]], axis=0) for d in dirs]

    stf_ref[...] = s0_ref[...].reshape(stf_ref.shape)
    stb_ref[...] = jnp.zeros_like(stb_ref)

    def step(t, carry):
        chains = []
        for d in range(2):
            for j in range(GLA_GROUP):
                idx = t * GLA_GROUP + j
                n = idx if d == 0 else n_chunks - 1 - idx
                chains.append({"d": d, "rows": pl.ds(pl.multiple_of(n * GLA_CHUNK, GLA_CHUNK), GLA_CHUNK)})
        for ch in chains:
            g = dirs[ch["d"]][0][0, ch["rows"], :]
            g_hi = g.astype(BF16)
            g_lo = (g - g_hi.astype(F32)).astype(BF16)
            ch["b2"] = _dot(tris[ch["d"]], jnp.concatenate([g_hi, g_lo], axis=1))
        for ch in chains:
            last = dirs[ch["d"]][2]
            b = ch["b2"][:, :LANES] + ch["b2"][:, LANES:]
            b_last = b[last:last + 1, :]
            q_e = gq_ref[0, ch["rows"], :] * jnp.exp(b)
            k = gk_ref[0, ch["rows"], :]
            ch["qm2"] = jnp.concatenate([q_e * masks[0], q_e * masks[1]], axis=0).astype(BF16)
            ch["k_e"] = (k * jnp.exp(-b)).astype(BF16)
            ch["k_d"] = (k * jnp.exp(b_last - b)).astype(BF16)
            ch["decay"] = jnp.exp(b_last)
        for ch in chains:
            ch["a2"] = _dot_nt(ch["qm2"], ch["k_e"])
            ch["kv"] = _dot_tn(gv_ref[0, ch["rows"], :], ch["k_d"])
        for d in range(2):
            st_ref = dirs[d][3]
            st = st_ref[...]
            for ch in chains:
                if ch["d"] == d:
                    ch["st_prev"] = st.astype(BF16)
                    st = ch["decay"] * st + ch["kv"]
            st_ref[...] = st
        for ch in chains:
            d = ch["d"]
            a2 = jnp.where(keep2[d], ch["a2"], 0.0).astype(BF16)
            for hh in range(2):
                rows_h = slice(GLA_CHUNK * hh, GLA_CHUNK * (hh + 1))
                cols = slice(GLA_DV * hh, GLA_DV * (hh + 1))
                o = (_dot(a2[rows_h], gv_ref[0, ch["rows"], cols])
                     + _dot_nt(ch["qm2"][rows_h], ch["st_prev"][cols]))
                dirs[d][4][ch["rows"], cols] = o
        return carry

    lax.fori_loop(0, n_chunks // GLA_GROUP, step, 0)

    def finish(i, carry):
        rows = pl.ds(pl.multiple_of(i * GLA_FINAL_ROWS, GLA_FINAL_ROWS), GLA_FINAL_ROWS)
        for hh in range(2):
            cols = slice(GLA_DV * hh, GLA_DV * (hh + 1))
            o = _rms(ofwd_ref[rows, cols] + obwd_ref[rows, cols], gain_ref[...])
            o_ref[0, rows, cols] = (o * gr_ref[0, rows, cols]).astype(BF16)
        return carry

    lax.fori_loop(0, seq // GLA_FINAL_ROWS, finish, 0)


def _gla(gq, gk, gv, gf, gb, gr, s0, gain):
    b, n, _ = gq.shape
    assert (n // GLA_CHUNK) % GLA_GROUP == 0 and n % GLA_FINAL_ROWS == 0
    n_pairs = GLA_HEADS // 2
    key_spec = pl.BlockSpec((1, n, LANES), lambda bi, pi: (bi, 0, pi))
    val_spec = pl.BlockSpec((1, n, 2 * GLA_DV), lambda bi, pi: (bi, 0, pi))
    return pl.pallas_call(
        _gla_kernel,
        grid=(b, n_pairs),
        in_specs=[key_spec, key_spec, val_spec, key_spec, key_spec, val_spec,
                  pl.BlockSpec((2, GLA_DV, LANES), lambda bi, pi: (pi, 0, 0)),
                  pl.BlockSpec((1, GLA_DV), lambda bi, pi: (0, 0))],
        out_specs=val_spec,
        out_shape=jax.ShapeDtypeStruct((b, n, GLA_VAL_WIDTH), BF16),
        scratch_shapes=[pltpu.VMEM((n, 2 * GLA_DV), F32), pltpu.VMEM((n, 2 * GLA_DV), F32),
                        pltpu.VMEM((2 * GLA_DV, LANES), F32), pltpu.VMEM((2 * GLA_DV, LANES), F32)],
        compiler_params=pltpu.CompilerParams(dimension_semantics=("arbitrary", "arbitrary"),
                                             vmem_limit_bytes=VMEM_LIMIT),
        name="gla",
    )(gq, gk, gv, gf, gb, gr, s0, gain)


def _out_ffn_kernel(x_ref, na_ref, gla_ref, wout_ref, gffn_ref, wg_ref, wu_ref, wd_ref, gfin_ref, o_ref):
    mix = _dot(na_ref[...], wout_ref[0:NA_WIDTH, :]) + _dot(gla_ref[...], wout_ref[NA_WIDTH:, :])
    h = x_ref[...] + mix
    hn = _rms(h, gffn_ref[...]).astype(BF16)
    act = (_silu(_dot(hn, wg_ref[...])) * _dot(hn, wu_ref[...])).astype(BF16)
    h = h + _dot(act, wd_ref[...])
    o_ref[...] = _rms(h, gfin_ref[...])


def _out_ffn(x2d, na2d, gla2d, w_out, g_ffn, w_gate, w_up, w_down, g_fin, tm):
    m, d = x2d.shape
    const = lambda shape: pl.BlockSpec(shape, lambda i: (0, 0))
    row = lambda width: pl.BlockSpec((tm, width), lambda i: (i, 0))
    return pl.pallas_call(
        _out_ffn_kernel,
        grid=(m // tm,),
        in_specs=[row(d), row(NA_WIDTH), row(GLA_VAL_WIDTH), const(w_out.shape), const((1, d)),
                  const(w_gate.shape), const(w_up.shape), const(w_down.shape), const((1, d))],
        out_specs=row(d),
        out_shape=jax.ShapeDtypeStruct((m, d), F32),
        compiler_params=pltpu.CompilerParams(dimension_semantics=("arbitrary",),
                                             vmem_limit_bytes=56 * 1024 * 1024),
        name="out_ffn",
    )(x2d, na2d, gla2d, w_out, g_ffn, w_gate, w_up, w_down, g_fin)


def _na_bias_pairs(rpb):
    cols = np.arange(GRID_W)
    col_start = np.clip(cols - NA_KW // 2, 0, GRID_W - NA_KW)
    valid = (cols[None, :] >= col_start[:, None]) & (cols[None, :] < col_start[:, None] + NA_KW)
    dc = np.clip(cols[None, :] - cols[:, None], -(NA_KW - 1), NA_KW - 1) + (NA_KW - 1)
    onehot = (dc[None] == np.arange(2 * NA_KW - 1)[:, None, None]).astype(np.float32)
    expanded = jnp.einsum("hro,oqk->hrqk", rpb, onehot, precision=lax.Precision.HIGHEST)
    tiles = jnp.where(valid[None, None], expanded * LOG2_E, NEG_INF).astype(F32)
    two_rows = jnp.concatenate([tiles[:, :-1], tiles[:, 1:]], axis=-1)
    n_dr = two_rows.shape[1]
    stacked = two_rows.reshape(NA_HEADS // 2, 2, n_dr, GRID_W, LANES).transpose(0, 2, 1, 3, 4)
    return stacked.reshape(NA_HEADS // 2, n_dr, 2 * GRID_W, LANES)


def _na_meta_bias_pairs(meta_bias):
    return jnp.repeat(meta_bias.reshape(NA_HEADS // 2, 2, N_META) * LOG2_E, GRID_W, axis=1)


def kernel(x, meta_tokens, norm_mix_gain, w_in, rpb, meta_bias, w_gate_up_fwd, b_gate_fwd, w_gate_up_bwd, b_gate_bwd, gla_norm_gain, w_out, norm_ffn_gain, w_ffn_gate, w_ffn_up, w_ffn_down, norm_final_gain):
    bsz, seq, d = x.shape
    assert w_in.shape[0] == 1, "single-layer block: meta-token outputs are never consumed"
    assert w_in.shape[2] == IN_WIDTH and seq % GRID_W == 0 and GLA_CHUNK == GRID_W

    w_pad = jnp.concatenate([w_in[0], jnp.zeros((d, IN_WIDTH_PAD - IN_WIDTH), F32)], axis=1).astype(BF16)
    zeros_up = jnp.zeros((GATE_SLAB - 2 * GLA_GATE_RANK, GLA_KEY_WIDTH), F32)
    wupf = jnp.concatenate([w_gate_up_fwd[0], jnp.zeros_like(w_gate_up_bwd[0]), zeros_up], axis=0).astype(BF16)
    wupb = jnp.concatenate([jnp.zeros_like(w_gate_up_fwd[0]), w_gate_up_bwd[0], zeros_up], axis=0).astype(BF16)
    bgf = b_gate_fwd[0][None, :]
    bgb = b_gate_bwd[0][None, :]
    gain_mix = norm_mix_gain[0][None, :]

    x2d = x.reshape(bsz * seq, d)
    proj_args = (gain_mix, w_pad, wupf, bgf, wupb, bgb)
    naq, nak, nav, gq, gk, gv, gr, gf, gb = _in_proj(x2d, *proj_args, tm=512)
    _, km, vm, _, gk_m, gv_m, _, gf_m, _ = _in_proj(meta_tokens, *proj_args, tm=N_META)

    s0 = _gla_init(gk_m, gv_m, gf_m)

    per_batch = lambda a: a.reshape(bsz, seq, a.shape[-1])
    na_out = _na(per_batch(naq), per_batch(nak), per_batch(nav), km, vm,
                 _na_bias_pairs(rpb[0]), _na_meta_bias_pairs(meta_bias[0]))
    gla_out = _gla(per_batch(gq), per_batch(gk), per_batch(gv), per_batch(gf), per_batch(gb),
                   per_batch(gr), s0, gla_norm_gain[0][None, :])

    out = _out_ffn(x2d, na_out.reshape(bsz * seq, NA_WIDTH), gla_out.reshape(bsz * seq, GLA_VAL_WIDTH),
                   w_out[0].astype(BF16), norm_ffn_gain[0][None, :], w_ffn_gate[0].astype(BF16),
                   w_ffn_up[0].astype(BF16), w_ffn_down[0].astype(BF16), norm_final_gain[None, :], tm=256)
    return out.reshape(bsz, seq, d)
```

```python
import functools

import jax
import jax.numpy as jnp
import numpy as np
from jax import lax
from jax.experimental import pallas as pl
from jax.experimental.pallas import tpu as pltpu

F32 = jnp.float32
BF16 = jnp.bfloat16

N_META = 16
GRID_W = 64
NA_HEADS = 8
NA_HEAD_DIM = 64
NA_WIDTH = NA_HEADS * NA_HEAD_DIM
NA_KH_MAX = 8
NA_KW = 16
GLA_HEADS = 4
GLA_DK = 64
GLA_DV = 128
GLA_KEY_WIDTH = GLA_HEADS * GLA_DK
GLA_VAL_WIDTH = GLA_HEADS * GLA_DV
GLA_GATE_RANK = 16
GLA_GATE_TAU = 16.0
GLA_CHUNK = 64
RMS_EPS = 1e-6
NEG_INF = -1e30
LOG2_E = 1.4426950408889634
NA_Q_SCALE = NA_HEAD_DIM ** -0.5 * LOG2_E

LANES = 128
GATE_SLAB = LANES
VMEM_LIMIT = 48 * 1024 * 1024
MXU_TILE = 256
FFN_CHUNK = 4 * MXU_TILE
SUB_ROWS = 256

_OFF_NA_Q = 0
_OFF_NA_K = _OFF_NA_Q + NA_WIDTH
_OFF_NA_V = _OFF_NA_K + NA_WIDTH
_OFF_G_Q = _OFF_NA_V + NA_WIDTH
_OFF_G_K = _OFF_G_Q + GLA_KEY_WIDTH
_OFF_G_V = _OFF_G_K + GLA_KEY_WIDTH
_OFF_G_R = _OFF_G_V + GLA_VAL_WIDTH
_OFF_GATE = _OFF_G_R + GLA_VAL_WIDTH
IN_WIDTH = _OFF_GATE + 2 * GLA_GATE_RANK
IN_WIDTH_PAD = _OFF_GATE + GATE_SLAB


def _rms(x, gain):
    ms = jnp.mean(x * x, axis=-1, keepdims=True)
    return x * lax.rsqrt(ms + RMS_EPS) * gain


def _silu(x):
    return x * (1.0 / (1.0 + jnp.exp(-x)))


def _log_sigmoid(x):
    return jnp.minimum(x, 0.0) - jnp.log1p(jnp.exp(-jnp.abs(x)))


def _dot(a, b):
    return jnp.dot(a, b, preferred_element_type=F32)


def _dot_nt(a, b):
    return lax.dot_general(a, b, (((1,), (1,)), ((), ())), preferred_element_type=F32)


def _dot_tn(a, b):
    return lax.dot_general(a, b, (((0,), (0,)), ((), ())), preferred_element_type=F32)


def _head_masks(rows, dtype):
    lane = lax.broadcasted_iota(jnp.int32, (rows, LANES), 1)
    lo = jnp.where(lane < LANES // 2, 1.0, 0.0).astype(dtype)
    hi = jnp.where(lane >= LANES // 2, 1.0, 0.0).astype(dtype)
    return lo, hi


def _in_proj_kernel(x_ref, gain_ref, w_ref, wupf_ref, bf_ref, wupb_ref, bb_ref,
                    naq_ref, nak_ref, nav_ref, gq_ref, gk_ref, gv_ref, gr_ref, gf_ref, gb_ref):
    tm = x_ref.shape[0]
    sub = min(tm, 2 * SUB_ROWS)
    subs = [slice(r, r + sub) for r in range(0, tm, sub)]
    xns = [_rms(x_ref[rows, :], gain_ref[...]).astype(BF16) for rows in subs]
    for rows, xn in zip(subs, xns):

        def proj(lo, width):
            return _dot(xn, w_ref[:, lo:lo + width])

        low = proj(_OFF_GATE, GATE_SLAB).astype(BF16)
        gq_ref[rows, :] = proj(_OFF_G_Q, GLA_KEY_WIDTH) * (GLA_DK ** -0.5)
        gr_ref[rows, :] = _silu(proj(_OFF_G_R, GLA_VAL_WIDTH))
        pre_f = _dot(low, wupf_ref[...]) + bf_ref[...]
        pre_b = _dot(low, wupb_ref[...]) + bb_ref[...]
        gf_ref[rows, :] = _log_sigmoid(pre_f) * (1.0 / GLA_GATE_TAU)
        gb_ref[rows, :] = _log_sigmoid(pre_b) * (1.0 / GLA_GATE_TAU)
        gk_ref[rows, :] = proj(_OFF_G_K, GLA_KEY_WIDTH)
        naq_ref[rows, :] = (proj(_OFF_NA_Q, NA_WIDTH) * NA_Q_SCALE).astype(BF16)
        gv_ref[rows, :] = proj(_OFF_G_V, GLA_VAL_WIDTH).astype(BF16)
        nak_ref[rows, :] = proj(_OFF_NA_K, NA_WIDTH).astype(BF16)
        nav_ref[rows, :] = proj(_OFF_NA_V, NA_WIDTH).astype(BF16)


def _in_proj(x2d, gain, w_pad, wupf, bgf, wupb, bgb, tm):
    m, d = x2d.shape
    const = lambda shape: pl.BlockSpec(shape, lambda i: (0, 0))
    row = lambda width: pl.BlockSpec((tm, width), lambda i: (i, 0))
    widths =(NA_WIDTH, NA_WIDTH, NA_WIDTH, GLA_KEY_WIDTH, GLA_KEY_WIDTH,
              GLA_VAL_WIDTH, GLA_VAL_WIDTH, GLA_KEY_WIDTH, GLA_KEY_WIDTH)
    dtypes = (BF16, BF16, BF16, F32, F32, BF16, F32, F32, F32)
    return pl.pallas_call(
        _in_proj_kernel,
        grid=(m // tm,),
        in_specs=[row(d), const((1, d)), const(w_pad.shape), const(wupf.shape), const(bgf.shape),
                  const(wupb.shape), const(bgb.shape)],
        out_specs=[row(w) for w in widths],
        out_shape=[jax.ShapeDtypeStruct((m, w), dt) for w, dt in zip(widths, dtypes)],
        compiler_params=pltpu.CompilerParams(dimension_semantics=("arbitrary",),
                                             vmem_limit_bytes=VMEM_LIMIT),
        name="in_proj",
    )(x2d, gain, w_pad, wupf, bgf, wupb, bgb)


def _gla_init_kernel(gk_ref, gv_ref, gf_ref, st_ref):
    n = gk_ref.shape[0]
    r = lax.broadcasted_iota(jnp.int32, (n, n), 0)
    c = lax.broadcasted_iota(jnp.int32, (n, n), 1)
    later = jnp.where(c > r, 1.0, 0.0).astype(BF16)
    g = gf_ref[...]
    g_hi = g.astype(BF16)
    g_lo = (g - g_hi.astype(F32)).astype(BF16)
    tail = _dot(later, g_hi) + _dot(later, g_lo)
    k_d = (gk_ref[...] * jnp.exp(tail)).astype(BF16)
    for h in range(GLA_HEADS):
        p = h // 2
        st_ref[h] = _dot_tn(gv_ref[:, GLA_DV * h:GLA_DV * (h + 1)], k_d[:, LANES * p:LANES * (p + 1)])


def _gla_init(gk_m, gv_m, gf_m):
    return pl.pallas_call(
        _gla_init_kernel,
        out_shape=jax.ShapeDtypeStruct((GLA_HEADS, GLA_DV, LANES), F32),
        name="gla_init",
    )(gk_m, gv_m, gf_m)


def _na_kernel(q_ref, k_ref, v_ref, km_ref, vm_ref, bias_ref, mb_ref, o_ref, sw_ref, sm_ref, *, n_rows):
    kh = min(NA_KH_MAX, n_rows)
    n_pairs = NA_HEADS // 2
    pair_lanes = [slice(LANES * p, LANES * (p + 1)) for p in range(n_pairs)]
    masks = _head_masks(GRID_W, BF16)
    sel_lo = lax.broadcasted_iota(jnp.int32, (GRID_W, LANES), 1) < LANES // 2

    def window_start(i):
        return jnp.clip(i - kh // 2, 0, n_rows - kh)

    def row_scores(i, slot):
        qoff = pl.multiple_of(i * GRID_W, GRID_W)
        koff = pl.multiple_of(window_start(i) * GRID_W, GRID_W)
        for p in range(n_pairs):
            q_pair = q_ref[0, pl.ds(qoff, GRID_W), pair_lanes[p]]
            q2 = jnp.concatenate([q_pair * masks[0], q_pair * masks[1]], axis=0)
            k_win = k_ref[0, pl.ds(koff, kh * GRID_W), pair_lanes[p]]
            sw_ref[slot, p] = _dot_nt(q2, k_win)
            sm_ref[slot, p] = _dot_nt(q2, km_ref[:, pair_lanes[p]])

    def row_finish(i, slot):
        s = window_start(i)
        dr0 = s - i + (NA_KH_MAX - 1)
        qoff = pl.multiple_of(i * GRID_W, GRID_W)
        koff = pl.multiple_of(s * GRID_W, GRID_W)
        biased = []
        for p in range(n_pairs):
            s_meta = sm_ref[slot, p] + mb_ref[p]
            chunks = [sw_ref[slot, p, :, LANES * t:LANES * (t + 1)] + bias_ref[p, dr0 + 2 * t]
                      for t in range(kh * GRID_W // LANES)]
            m_el = functools.reduce(jnp.maximum, chunks)
            m = jnp.maximum(jnp.max(m_el, axis=-1, keepdims=True),
                            jnp.max(s_meta, axis=-1, keepdims=True))
            biased.append((chunks, s_meta, m))
        probs = []
        for p in range(n_pairs):
            chunks, s_meta, m = biased[p]
            p_chunks = [jnp.exp2(c - m) for c in chunks]
            p_meta = jnp.exp2(s_meta - m)
            l = (jnp.sum(functools.reduce(jnp.add, p_chunks), axis=-1, keepdims=True)
                 + jnp.sum(p_meta, axis=-1, keepdims=True))
            probs.append((jnp.concatenate(p_chunks, axis=1).astype(BF16), p_meta.astype(BF16), l))
        pair_outs = []
        for p in range(n_pairs):
            p_win, p_meta, l = probs[p]
            v_win = v_ref[0, pl.ds(koff, kh * GRID_W), pair_lanes[p]]
            o2 = (_dot(p_win, v_win) + _dot(p_meta, vm_ref[:, pair_lanes[p]])) / l
            pair_outs.append(jnp.where(sel_lo, o2[:GRID_W], o2[GRID_W:]).astype(BF16))
        o_ref[0, pl.ds(qoff, GRID_W), :] = jnp.concatenate(pair_outs, axis=1)

    def two_rows(j, carry):
        a = 2 * j
        row_scores(a + 1, 1)
        row_finish(a, 0)
        row_scores(jnp.minimum(a + 2, n_rows - 1), 0)
        row_finish(a + 1, 1)
        return carry

    row_scores(0, 0)
    lax.fori_loop(0, n_rows // 2, two_rows, 0)


def _na(q, k, v, k_meta, v_meta, bias_pairs, meta_bias):
    b, n, w = q.shape
    n_rows = n // GRID_W
    assert n_rows % 2 == 0
    kh = min(NA_KH_MAX, n_rows)
    kern = functools.partial(_na_kernel, n_rows=n_rows)
    batch_spec = pl.BlockSpec((1, n, w), lambda bi: (bi, 0, 0))
    whole = lambda a: pl.BlockSpec(a.shape, lambda bi: (0,) * a.ndim)
    return pl.pallas_call(
        kern,
        grid=(b,),
        in_specs=[batch_spec, batch_spec, batch_spec, whole(k_meta), whole(v_meta), whole(bias_pairs),
                  whole(meta_bias)],
        out_specs=batch_spec,
        out_shape=jax.ShapeDtypeStruct((b, n, w), BF16),
        scratch_shapes=[pltpu.VMEM((2, NA_HEADS // 2, 2 * GRID_W, kh * GRID_W), F32),
                        pltpu.VMEM((2, NA_HEADS // 2, 2 * GRID_W, N_META), F32)],
        compiler_params=pltpu.CompilerParams(dimension_semantics=("arbitrary",),
                                             vmem_limit_bytes=VMEM_LIMIT),
        name="na",
    )(q, k, v, k_meta, v_meta, bias_pairs, meta_bias)


GLA_GROUP = 4
GLA_FINAL_ROWS = 256


def _gla_kernel(gq_ref, gk_ref, gv_ref, gf_ref, gb_ref, gr_ref, s0_ref, gain_ref, o_ref,
                ofwd_ref, obwd_ref, stf_ref, stb_ref):
    seq = gq_ref.shape[1]
    n_chunks = seq // GLA_CHUNK
    r = lax.broadcasted_iota(jnp.int32, (GLA_CHUNK, GLA_CHUNK), 0)
    c = lax.broadcasted_iota(jnp.int32, (GLA_CHUNK, GLA_CHUNK), 1)
    masks = _head_masks(GLA_CHUNK, F32)
    dirs = ((gf_ref, c <= r, GLA_CHUNK - 1, stf_ref, ofwd_ref),
            (gb_ref, c >= r, 0, stb_ref, obwd_ref))
    tris = [jnp.where(d[1], 1.0, 0.0).astype(BF16) for d in dirs]
    keep2 = [jnp.concatenate([d[1], d[1]], axis=0) for d in dirs]

    stf_ref[...] = s0_ref[...].reshape(stf_ref.shape)
    stb_ref[...] = jnp.zeros_like(stb_ref)

    def step(t, carry):
        chains = []
        for d in range(2):
            for j in range(GLA_GROUP):
                idx = t * GLA_GROUP + j
                n = idx if d == 0 else n_chunks - 1 - idx
                chains.append({"d": d, "rows": pl.ds(pl.multiple_of(n * GLA_CHUNK, GLA_CHUNK), GLA_CHUNK)})
        for ch in chains:
            g = dirs[ch["d"]][0][0, ch["rows"], :]
            g_hi = g.astype(BF16)
            g_lo = (g - g_hi.astype(F32)).astype(BF16)
            ch["b2"] = _dot(tris[ch["d"]], jnp.concatenate([g_hi, g_lo], axis=1))
        for ch in chains:
            last = dirs[ch["d"]][2]
            b = ch["b2"][:, :LANES] + ch["b2"][:, LANES:]
            b_last = b[last:last + 1, :]
            q_e = gq_ref[0, ch["rows"], :] * jnp.exp(b)
            k = gk_ref[0, ch["rows"], :]
            ch["qm2"] = jnp.concatenate([q_e * masks[0], q_e * masks[1]], axis=0).astype(BF16)
            ch["k_e"] = (k * jnp.exp(-b)).astype(BF16)
            ch["k_d"] = (k * jnp.exp(b_last - b)).astype(BF16)
            ch["decay"] = jnp.exp(b_last)
        for ch in chains:
            ch["a2"] = _dot_nt(ch["qm2"], ch["k_e"])
            ch["kv"] = _dot_tn(gv_ref[0, ch["rows"], :], ch["k_d"])
        for d in range(2):
            st_ref = dirs[d][3]
            st = st_ref[...]
            for ch in chains:
                if ch["d"] == d:
                    ch["st_prev"] = st.astype(BF16)
                    st = ch["decay"] * st + ch["kv"]
            st_ref[...] = st
        for ch in chains:
            d = ch["d"]
            a2 = jnp.where(keep2[d], ch["a2"], 0.0).astype(BF16)
            for hh in range(2):
                rows_h = slice(GLA_CHUNK * hh, GLA_CHUNK * (hh + 1))
                cols = slice(GLA_DV * hh, GLA_DV * (hh + 1))
                o = (_dot(a2[rows_h], gv_ref[0, ch["rows"], cols])
                     + _dot_nt(ch["qm2"][rows_h], ch["st_prev"][cols]))
                dirs[d][4][ch["rows"], cols] = o
        return carry

    lax.fori_loop(0, n_chunks // GLA_GROUP, step, 0)

    def finish(i, carry):
        rows = pl.ds(pl.multiple_of(i * GLA_FINAL_ROWS, GLA_FINAL_ROWS), GLA_FINAL_ROWS)
        for hh in range(2):
            cols = slice(GLA_DV * hh, GLA_DV * (hh + 1))
            o = _rms(ofwd_ref[rows, cols] + obwd_ref[rows, cols], gain_ref[...])
            o_ref[0, rows, cols] = (o * gr_ref[0, rows, cols]).astype(BF16)
        return carry

    lax.fori_loop(0, seq // GLA_FINAL_ROWS, finish, 0)


def _gla(gq, gk, gv, gf, gb, gr, s0, gain):
    b, n, _ = gq.shape
    assert (n // GLA_CHUNK) % GLA_GROUP == 0 and n % GLA_FINAL_ROWS == 0
    n_pairs = GLA_HEADS // 2
    key_spec = pl.BlockSpec((1, n, LANES), lambda bi, pi: (bi, 0, pi))
    val_spec = pl.BlockSpec((1, n, 2 * GLA_DV), lambda bi, pi: (bi, 0, pi))
    return pl.pallas_call(
        _gla_kernel,
        grid=(b, n_pairs),
        in_specs=[key_spec, key_spec, val_spec, key_spec, key_spec, val_spec,
                  pl.BlockSpec((2, GLA_DV, LANES), lambda bi, pi: (pi, 0, 0)),
                  pl.BlockSpec((1, GLA_DV), lambda bi, pi: (0, 0))],
        out_specs=val_spec,
        out_shape=jax.ShapeDtypeStruct((b, n, GLA_VAL_WIDTH), BF16),
        scratch_shapes=[pltpu.VMEM((n, 2 * GLA_DV), F32), pltpu.VMEM((n, 2 * GLA_DV), F32),
                        pltpu.VMEM((2 * GLA_DV, LANES), F32), pltpu.VMEM((2 * GLA_DV, LANES), F32)],
        compiler_params=pltpu.CompilerParams(dimension_semantics=("arbitrary", "arbitrary"),
                                             vmem_limit_bytes=VMEM_LIMIT),
        name="gla",
    )(gq, gk, gv, gf, gb, gr, s0, gain)


def _out_ffn_kernel(x_ref, na_ref, gla_ref, wout_ref, gffn_ref, wg_ref, wu_ref, wd_ref, gfin_ref, o_ref):
    mix = _dot(na_ref[...], wout_ref[0:NA_WIDTH, :]) + _dot(gla_ref[...], wout_ref[NA_WIDTH:, :])
    h = x_ref[...] + mix
    hn = _rms(h, gffn_ref[...]).astype(BF16)
    act = (_silu(_dot(hn, wg_ref[...])) * _dot(hn, wu_ref[...])).astype(BF16)
    h = h + _dot(act, wd_ref[...])
    o_ref[...] = _rms(h, gfin_ref[...])


def _out_ffn(x2d, na2d, gla2d, w_out, g_ffn, w_gate, w_up, w_down, g_fin, tm):
    m, d = x2d.shape
    const = lambda shape: pl.BlockSpec(shape, lambda i: (0, 0))
    row = lambda width: pl.BlockSpec((tm, width), lambda i: (i, 0))
    return pl.pallas_call(
        _out_ffn_kernel,
        grid=(m // tm,),
        in_specs=[row(d), row(NA_WIDTH), row(GLA_VAL_WIDTH), const(w_out.shape), const((1, d)),
                  const(w_gate.shape), const(w_up.shape), const(w_down.shape), const((1, d))],
        out_specs=row(d),
        out_shape=jax.ShapeDtypeStruct((m, d), F32),
        compiler_params=pltpu.CompilerParams(dimension_semantics=("arbitrary",),
                                             vmem_limit_bytes=56 * 1024 * 1024),
        name="out_ffn",
    )(x2d, na2d, gla2d, w_out, g_ffn, w_gate, w_up, w_down, g_fin)


def _na_bias_pairs(rpb):
    cols = np.arange(GRID_W)
    col_start = np.clip(cols - NA_KW // 2, 0, GRID_W - NA_KW)
    valid = (cols[None, :] >= col_start[:, None]) & (cols[None, :] < col_start[:, None] + NA_KW)
    dc = np.clip(cols[None, :] - cols[:, None], -(NA_KW - 1), NA_KW - 1) + (NA_KW - 1)
    onehot = (dc[None] == np.arange(2 * NA_KW - 1)[:, None, None]).astype(np.float32)
    expanded = jnp.einsum("hro,oqk->hrqk", rpb, onehot, precision=lax.Precision.HIGHEST)
    tiles = jnp.where(valid[None, None], expanded * LOG2_E, NEG_INF).astype(F32)
    two_rows = jnp.concatenate([tiles[:, :-1], tiles[:, 1:]], axis=-1)
    n_dr = two_rows.shape[1]
    stacked = two_rows.reshape(NA_HEADS // 2, 2, n_dr, GRID_W, LANES).transpose(0, 2, 1, 3, 4)
    return stacked.reshape(NA_HEADS // 2, n_dr, 2 * GRID_W, LANES)


def _na_meta_bias_pairs(meta_bias):
    return jnp.repeat(meta_bias.reshape(NA_HEADS // 2, 2, N_META) * LOG2_E, GRID_W, axis=1)


def kernel(x, meta_tokens, norm_mix_gain, w_in, rpb, meta_bias, w_gate_up_fwd, b_gate_fwd, w_gate_up_bwd, b_gate_bwd, gla_norm_gain, w_out, norm_ffn_gain, w_ffn_gate, w_ffn_up, w_ffn_down, norm_final_gain):
    bsz, seq, d = x.shape
    assert w_in.shape[0] == 1, "single-layer block: meta-token outputs are never consumed"
    assert w_in.shape[2] == IN_WIDTH and seq % GRID_W == 0 and GLA_CHUNK == GRID_W

    w_pad = jnp.concatenate([w_in[0], jnp.zeros((d, IN_WIDTH_PAD - IN_WIDTH), F32)], axis=1).astype(BF16)
    zeros_up = jnp.zeros((GATE_SLAB - 2 * GLA_GATE_RANK, GLA_KEY_WIDTH), F32)
    wupf = jnp.concatenate([w_gate_up_fwd[0], jnp.zeros_like(w_gate_up_bwd[0]), zeros_up], axis=0).astype(BF16)
    wupb = jnp.concatenate([jnp.zeros_like(w_gate_up_fwd[0]), w_gate_up_bwd[0], zeros_up], axis=0).astype(BF16)
    bgf = b_gate_fwd[0][None, :]
    bgb = b_gate_bwd[0][None, :]
    gain_mix = norm_mix_gain[0][None, :]

    x2d = x.reshape(bsz * seq, d)
    proj_args = (gain_mix, w_pad, wupf, bgf, wupb, bgb)
    naq, nak, nav, gq, gk, gv, gr, gf, gb = _in_proj(x2d, *proj_args, tm=512)
    _, km, vm, _, gk_m, gv_m, _, gf_m, _ = _in_proj(meta_tokens, *proj_args, tm=N_META)

    s0 = _gla_init(gk_m, gv_m, gf_m)

    per_batch = lambda a: a.reshape(bsz, seq, a.shape[-1])
    na_out = _na(per_batch(naq), per_batch(nak), per_batch(nav), km, vm,
                 _na_bias_pairs(rpb[0]), _na_meta_bias_pairs(meta_bias[0]))
    gla_out = _gla(per_batch(gq), per_batch(gk), per_batch(gv), per_batch(gf), per_batch(gb),
                   per_batch(gr), s0, gla_norm_gain[0][None, :])

    out = _out_ffn(x2d, na_out.reshape(bsz * seq, NA_WIDTH), gla_out.reshape(bsz * seq, GLA_VAL_WIDTH),
                   w_out[0].astype(BF16), norm_ffn_gain[0][None, :], w_ffn_gate[0].astype(BF16),
                   w_ffn_up[0].astype(BF16), w_ffn_down[0].astype(BF16), norm_final_gain[None, :], tm=256)
    return out.reshape(bsz, seq, d)
```

```python
import functools

import jax
import jax.numpy as jnp
import numpy as np
from jax import lax
from jax.experimental import pallas as pl
from jax.experimental.pallas import tpu as pltpu

F32 = jnp.float32
BF16 = jnp.bfloat16

N_META = 16
GRID_W = 64
NA_HEADS = 8
NA_HEAD_DIM = 64
NA_WIDTH = NA_HEADS * NA_HEAD_DIM
NA_KH_MAX = 8
NA_KW = 16
GLA_HEADS = 4
GLA_DK = 64
GLA_DV = 128
GLA_KEY_WIDTH = GLA_HEADS * GLA_DK
GLA_VAL_WIDTH = GLA_HEADS * GLA_DV
GLA_GATE_RANK = 16
GLA_GATE_TAU = 16.0
GLA_CHUNK = 64
RMS_EPS = 1e-6
NEG_INF = -1e30
LOG2_E = 1.4426950408889634
NA_Q_SCALE = NA_HEAD_DIM ** -0.5 * LOG2_E

LANES = 128
GATE_SLAB = LANES
VMEM_LIMIT = 48 * 1024 * 1024
MXU_TILE = 256
FFN_CHUNK = 4 * MXU_TILE
SUB_ROWS = 256

_OFF_NA_Q = 0
_OFF_NA_K = _OFF_NA_Q + NA_WIDTH
_OFF_NA_V = _OFF_NA_K + NA_WIDTH
_OFF_G_Q = _OFF_NA_V + NA_WIDTH
_OFF_G_K = _OFF_G_Q + GLA_KEY_WIDTH
_OFF_G_V = _OFF_G_K + GLA_KEY_WIDTH
_OFF_G_R = _OFF_G_V + GLA_VAL_WIDTH
_OFF_GATE = _OFF_G_R + GLA_VAL_WIDTH
IN_WIDTH = _OFF_GATE + 2 * GLA_GATE_RANK
IN_WIDTH_PAD = _OFF_GATE + GATE_SLAB


def _rms(x, gain):
    ms = jnp.mean(x * x, axis=-1, keepdims=True)
    return x * lax.rsqrt(ms + RMS_EPS) * gain


def _silu(x):
    return x * (1.0 / (1.0 + jnp.exp(-x)))


def _log_sigmoid(x):
    return jnp.minimum(x, 0.0) - jnp.log1p(jnp.exp(-jnp.abs(x)))


def _dot(a, b):
    return jnp.dot(a, b, preferred_element_type=F32)


def _dot_nt(a, b):
    return lax.dot_general(a, b, (((1,), (1,)), ((), ())), preferred_element_type=F32)


def _dot_tn(a, b):
    return lax.dot_general(a, b, (((0,), (0,)), ((), ())), preferred_element_type=F32)


def _head_masks(rows, dtype):
    lane = lax.broadcasted_iota(jnp.int32, (rows, LANES), 1)
    lo = jnp.where(lane < LANES // 2, 1.0, 0.0).astype(dtype)
    hi = jnp.where(lane >= LANES // 2, 1.0, 0.0).astype(dtype)
    return lo, hi


def _in_proj_kernel(x_ref, gain_ref, w_ref, wupf_ref, bf_ref, wupb_ref, bb_ref,
                    naq_ref, nak_ref, nav_ref, gq_ref, gk_ref, gv_ref, gr_ref, gf_ref, gb_ref):
    tm = x_ref.shape[0]
    sub = min(tm, 2 * SUB_ROWS)
    subs = [slice(r, r + sub) for r in range(0, tm, sub)]
    xns = [_rms(x_ref[rows, :], gain_ref[...]).astype(BF16) for rows in subs]
    for rows, xn in zip(subs, xns):

        def proj(lo, width):
            return _dot(xn, w_ref[:, lo:lo + width])

        low = proj(_OFF_GATE, GATE_SLAB).astype(BF16)
        gq_ref[rows, :] = proj(_OFF_G_Q, GLA_KEY_WIDTH) * (GLA_DK ** -0.5)
        gr_ref[rows, :] = _silu(proj(_OFF_G_R, GLA_VAL_WIDTH))
        pre_f = _dot(low, wupf_ref[...]) + bf_ref[...]
        pre_b = _dot(low, wupb_ref[...]) + bb_ref[...]
        gf_ref[rows, :] = _log_sigmoid(pre_f) * (1.0 / GLA_GATE_TAU)
        gb_ref[rows, :] = _log_sigmoid(pre_b) * (1.0 / GLA_GATE_TAU)
        gk_ref[rows, :] = proj(_OFF_G_K, GLA_KEY_WIDTH)
        naq_ref[rows, :] = (proj(_OFF_NA_Q, NA_WIDTH) * NA_Q_SCALE).astype(BF16)
        gv_ref[rows, :] = proj(_OFF_G_V, GLA_VAL_WIDTH).astype(BF16)
        nak_ref[rows, :] = proj(_OFF_NA_K, NA_WIDTH).astype(BF16)
        nav_ref[rows, :] = proj(_OFF_NA_V, NA_WIDTH).astype(BF16)


def _in_proj(x2d, gain, w_pad, wupf, bgf, wupb, bgb, tm):
    m, d = x2d.shape
    const = lambda shape: pl.BlockSpec(shape, lambda i: (0, 0))
    row = lambda width: pl.BlockSpec((tm, width), lambda i: (i, 0))
    widths =(NA_WIDTH, NA_WIDTH, NA_WIDTH, GLA_KEY_WIDTH, GLA_KEY_WIDTH,
              GLA_VAL_WIDTH, GLA_VAL_WIDTH, GLA_KEY_WIDTH, GLA_KEY_WIDTH)
    dtypes = (BF16, BF16, BF16, F32, F32, BF16, F32, F32, F32)
    return pl.pallas_call(
        _in_proj_kernel,
        grid=(m // tm,),
        in_specs=[row(d), const((1, d)), const(w_pad.shape), const(wupf.shape), const(bgf.shape),
                  const(wupb.shape), const(bgb.shape)],
        out_specs=[row(w) for w in widths],
        out_shape=[jax.ShapeDtypeStruct((m, w), dt) for w, dt in zip(widths, dtypes)],
        compiler_params=pltpu.CompilerParams(dimension_semantics=("arbitrary",),
                                             vmem_limit_bytes=VMEM_LIMIT),
        name="in_proj",
    )(x2d, gain, w_pad, wupf, bgf, wupb, bgb)


def _gla_init_kernel(gk_ref, gv_ref, gf_ref, st_ref):
    n = gk_ref.shape[0]
    r = lax.broadcasted_iota(jnp.int32, (n, n), 0)
    c = lax.broadcasted_iota(jnp.int32, (n, n), 1)
    later = jnp.where(c > r, 1.0, 0.0).astype(BF16)
    g = gf_ref[...]
    g_hi = g.astype(BF16)
    g_lo = (g - g_hi.astype(F32)).astype(BF16)
    tail = _dot(later, g_hi) + _dot(later, g_lo)
    k_d = (gk_ref[...] * jnp.exp(tail)).astype(BF16)
    for h in range(GLA_HEADS):
        p = h // 2
        st_ref[h] = _dot_tn(gv_ref[:, GLA_DV * h:GLA_DV * (h + 1)], k_d[:, LANES * p:LANES * (p + 1)])


def _gla_init(gk_m, gv_m, gf_m):
    return pl.pallas_call(
        _gla_init_kernel,
        out_shape=jax.ShapeDtypeStruct((GLA_HEADS, GLA_DV, LANES), F32),
        name="gla_init",
    )(gk_m, gv_m, gf_m)


def _na_kernel(q_ref, k_ref, v_ref, km_ref, vm_ref, bias_ref, mb_ref, o_ref, sw_ref, sm_ref, *, n_rows):
    kh = min(NA_KH_MAX, n_rows)
    n_pairs = NA_HEADS // 2
    pair_lanes = [slice(LANES * p, LANES * (p + 1)) for p in range(n_pairs)]
    masks = _head_masks(GRID_W, BF16)
    sel_lo = lax.broadcasted_iota(jnp.int32, (GRID_W, LANES), 1) < LANES // 2

    def window_start(i):
        return jnp.clip(i - kh // 2, 0, n_rows - kh)

    def row_scores(i, slot):
        qoff = pl.multiple_of(i * GRID_W, GRID_W)
        koff = pl.multiple_of(window_start(i) * GRID_W, GRID_W)
        for p in range(n_pairs):
            q_pair = q_ref[0, pl.ds(qoff, GRID_W), pair_lanes[p]]
            q2 = jnp.concatenate([q_pair * masks[0], q_pair * masks[1]], axis=0)
            k_win = k_ref[0, pl.ds(koff, kh * GRID_W), pair_lanes[p]]
            sw_ref[slot, p] = _dot_nt(q2, k_win)
            sm_ref[slot, p] = _dot_nt(q2, km_ref[:, pair_lanes[p]])

    def row_finish(i, slot):
        s = window_start(i)
        dr0 = s - i + (NA_KH_MAX - 1)
        qoff = pl.multiple_of(i * GRID_W, GRID_W)
        koff = pl.multiple_of(s * GRID_W, GRID_W)
        biased = []
        for p in range(n_pairs):
            s_meta = sm_ref[slot, p] + mb_ref[p]
            chunks = [sw_ref[slot, p, :, LANES * t:LANES * (t + 1)] + bias_ref[p, dr0 + 2 * t]
                      for t in range(kh * GRID_W // LANES)]
            m_el = functools.reduce(jnp.maximum, chunks)
            m = jnp.maximum(jnp.max(m_el, axis=-1, keepdims=True),
                            jnp.max(s_meta, axis=-1, keepdims=True))
            biased.append((chunks, s_meta, m))
        probs = []
        for p in range(n_pairs):
            chunks, s_meta, m = biased[p]
            p_chunks = [jnp.exp2(c - m) for c in chunks]
            p_meta = jnp.exp2(s_meta - m)
            l = (jnp.sum(functools.reduce(jnp.add, p_chunks), axis=-1, keepdims=True)
                 + jnp.sum(p_meta, axis=-1, keepdims=True))
            probs.append((jnp.concatenate(p_chunks, axis=1).astype(BF16), p_meta.astype(BF16), l))
        pair_outs = []
        for p in range(n_pairs):
            p_win, p_meta, l = probs[p]
            v_win = v_ref[0, pl.ds(koff, kh * GRID_W), pair_lanes[p]]
            o2 = (_dot(p_win, v_win) + _dot(p_meta, vm_ref[:, pair_lanes[p]])) / l
            pair_outs.append(jnp.where(sel_lo, o2[:GRID_W], o2[GRID_W:]).astype(BF16))
        o_ref[0, pl.ds(qoff, GRID_W), :] = jnp.concatenate(pair_outs, axis=1)

    def two_rows(j, carry):
        a = 2 * j
        row_scores(a + 1, 1)
        row_finish(a, 0)
        row_scores(jnp.minimum(a + 2, n_rows - 1), 0)
        row_finish(a + 1, 1)
        return carry

    row_scores(0, 0)
    lax.fori_loop(0, n_rows // 2, two_rows, 0)


def _na(q, k, v, k_meta, v_meta, bias_pairs, meta_bias):
    b, n, w = q.shape
    n_rows = n // GRID_W
    assert n_rows % 2 == 0
    kh = min(NA_KH_MAX, n_rows)
    kern = functools.partial(_na_kernel, n_rows=n_rows)
    batch_spec = pl.BlockSpec((1, n, w), lambda bi: (bi, 0, 0))
    whole = lambda a: pl.BlockSpec(a.shape, lambda bi: (0,) * a.ndim)
    return pl.pallas_call(
        kern,
        grid=(b,),
        in_specs=[batch_spec, batch_spec, batch_spec, whole(k_meta), whole(v_meta), whole(bias_pairs),
                  whole(meta_bias)],
        out_specs=batch_spec,
        out_shape=jax.ShapeDtypeStruct((b, n, w), BF16),
        scratch_shapes=[pltpu.VMEM((2, NA_HEADS // 2, 2 * GRID_W, kh * GRID_W), F32),
                        pltpu.VMEM((2, NA_HEADS // 2, 2 * GRID_W, N_META), F32)],
        compiler_params=pltpu.CompilerParams(dimension_semantics=("arbitrary",),
                                             vmem_limit_bytes=VMEM_LIMIT),
        name="na",
    )(q, k, v, k_meta, v_meta, bias_pairs, meta_bias)


GLA_GROUP = 8
GLA_FINAL_ROWS = 256


def _gla_kernel(gq_ref, gk_ref, gv_ref, gf_ref, gb_ref, gr_ref, s0_ref, gain_ref, o_ref,
                ofwd_ref, obwd_ref, stf_ref, stb_ref):
    seq = gq_ref.shape[1]
    n_chunks = seq // GLA_CHUNK
    r = lax.broadcasted_iota(jnp.int32, (GLA_CHUNK, GLA_CHUNK), 0)
    c = lax.broadcasted_iota(jnp.int32, (GLA_CHUNK, GLA_CHUNK), 1)
    masks = _head_masks(GLA_CHUNK, F32)
    dirs = ((gf_ref, c <= r, GLA_CHUNK - 1, stf_ref, ofwd_ref),
            (gb_ref, c >= r, 0, stb_ref, obwd_ref))
    tris = [jnp.where(d[1], 1.0, 0.0).astype(BF16) for d in dirs]
    keep2 = [jnp.concatenate([d[1], d[1]], axis=0) for d in dirs]

    stf_ref[...] = s0_ref[...].reshape(stf_ref.shape)
    stb_ref[...] = jnp.zeros_like(stb_ref)

    def step(t, carry):
        chains = []
        for d in range(2):
            for j in range(GLA_GROUP):
                idx = t * GLA_GROUP + j
                n = idx if d == 0 else n_chunks - 1 - idx
                chains.append({"d": d, "rows": pl.ds(pl.multiple_of(n * GLA_CHUNK, GLA_CHUNK), GLA_CHUNK)})
        for ch in chains:
            g = dirs[ch["d"]][0][0, ch["rows"], :]
            g_hi = g.astype(BF16)
            g_lo = (g - g_hi.astype(F32)).astype(BF16)
            ch["b2"] = _dot(tris[ch["d"]], jnp.concatenate([g_hi, g_lo], axis=1))
        for ch in chains:
            last = dirs[ch["d"]][2]
            b = ch["b2"][:, :LANES] + ch["b2"][:, LANES:]
            b_last = b[last:last + 1, :]
            q_e = gq_ref[0, ch["rows"], :] * jnp.exp(b)
            k = gk_ref[0, ch["rows"], :]
            ch["qm2"] = jnp.concatenate([q_e * masks[0], q_e * masks[1]], axis=0).astype(BF16)
            ch["k_e"] = (k * jnp.exp(-b)).astype(BF16)
            ch["k_d"] = (k * jnp.exp(b_last - b)).astype(BF16)
            ch["decay"] = jnp.exp(b_last)
        for ch in chains:
            ch["a2"] = _dot_nt(ch["qm2"], ch["k_e"])
            ch["kv"] = _dot_tn(gv_ref[0, ch["rows"], :], ch["k_d"])
        for d in range(2):
            st_ref = dirs[d][3]
            st = st_ref[...]
            for ch in chains:
                if ch["d"] == d:
                    ch["st_prev"] = st.astype(BF16)
                    st = ch["decay"] * st + ch["kv"]
            st_ref[...] = st
        for ch in chains:
            d = ch["d"]
            a2 = jnp.where(keep2[d], ch["a2"], 0.0).astype(BF16)
            for hh in range(2):
                rows_h = slice(GLA_CHUNK * hh, GLA_CHUNK * (hh + 1))
                cols = slice(GLA_DV * hh, GLA_DV * (hh + 1))
                o = (_dot(a2[rows_h], gv_ref[0, ch["rows"], cols])
                     + _dot_nt(ch["qm2"][rows_h], ch["st_prev"][cols]))
                dirs[d][4][ch["rows"], cols] = o
        return carry

    lax.fori_loop(0, n_chunks // GLA_GROUP, step, 0)

    def finish(i, carry):
        rows = pl.ds(pl.multiple_of(i * GLA_FINAL_ROWS, GLA_FINAL_ROWS), GLA_FINAL_ROWS)
        for hh in range(2):
            cols = slice(GLA_DV * hh, GLA_DV * (hh + 1))
            o = _rms(ofwd_ref[rows, cols] + obwd_ref[rows, cols], gain_ref[...])
            o_ref[0, rows, cols] = (o * gr_ref[0, rows, cols]).astype(BF16)
        return carry

    lax.fori_loop(0, seq // GLA_FINAL_ROWS, finish, 0)


def _gla(gq, gk, gv, gf, gb, gr, s0, gain):
    b, n, _ = gq.shape
    assert (n // GLA_CHUNK) % GLA_GROUP == 0 and n % GLA_FINAL_ROWS == 0
    n_pairs = GLA_HEADS // 2
    key_spec = pl.BlockSpec((1, n, LANES), lambda bi, pi: (bi, 0, pi))
    val_spec = pl.BlockSpec((1, n, 2 * GLA_DV), lambda bi, pi: (bi, 0, pi))
    return pl.pallas_call(
        _gla_kernel,
        grid=(b, n_pairs),
        in_specs=[key_spec, key_spec, val_spec, key_spec, key_spec, val_spec,
                  pl.BlockSpec((2, GLA_DV, LANES), lambda bi, pi: (pi, 0, 0)),
                  pl.BlockSpec((1, GLA_DV), lambda bi, pi: (0, 0))],
        out_specs=val_spec,
        out_shape=jax.ShapeDtypeStruct((b, n, GLA_VAL_WIDTH), BF16),
        scratch_shapes=[pltpu.VMEM((n, 2 * GLA_DV), F32), pltpu.VMEM((n, 2 * GLA_DV), F32),
                        pltpu.VMEM((2 * GLA_DV, LANES), F32), pltpu.VMEM((2 * GLA_DV, LANES), F32)],
        compiler_params=pltpu.CompilerParams(dimension_semantics=("arbitrary", "arbitrary"),
                                             vmem_limit_bytes=VMEM_LIMIT),
        name="gla",
    )(gq, gk, gv, gf, gb, gr, s0, gain)


def _out_ffn_kernel(x_ref, na_ref, gla_ref, wout_ref, gffn_ref, wg_ref, wu_ref, wd_ref, gfin_ref, o_ref):
    subs = [slice(r, r + SUB_ROWS) for r in range(0, x_ref.shape[0], SUB_ROWS)]
    mixes = [_dot(na_ref[rows, :], wout_ref[0:NA_WIDTH, :]) + _dot(gla_ref[rows, :], wout_ref[NA_WIDTH:, :])
             for rows in subs]
    hs = [x_ref[rows, :] + mix for rows, mix in zip(subs, mixes)]
    hns = [_rms(h, gffn_ref[...]).astype(BF16) for h in hs]
    acts = [(_silu(_dot(hn, wg_ref[...])) * _dot(hn, wu_ref[...])).astype(BF16) for hn in hns]
    outs = [h + _dot(act, wd_ref[...]) for h, act in zip(hs, acts)]
    for rows, out in zip(subs, outs):
        o_ref[rows, :] = _rms(out, gfin_ref[...])


def _out_ffn(x2d, na2d, gla2d, w_out, g_ffn, w_gate, w_up, w_down, g_fin, tm):
    m, d = x2d.shape
    const = lambda shape: pl.BlockSpec(shape, lambda i: (0, 0))
    row = lambda width: pl.BlockSpec((tm, width), lambda i: (i, 0))
    return pl.pallas_call(
        _out_ffn_kernel,
        grid=(m // tm,),
        in_specs=[row(d), row(NA_WIDTH), row(GLA_VAL_WIDTH), const(w_out.shape), const((1, d)),
                  const(w_gate.shape), const(w_up.shape), const(w_down.shape), const((1, d))],
        out_specs=row(d),
        out_shape=jax.ShapeDtypeStruct((m, d), F32),
        compiler_params=pltpu.CompilerParams(dimension_semantics=("arbitrary",),
                                             vmem_limit_bytes=56 * 1024 * 1024),
        name="out_ffn",
    )(x2d, na2d, gla2d, w_out, g_ffn, w_gate, w_up, w_down, g_fin)


def _na_bias_pairs(rpb):
    cols = np.arange(GRID_W)
    col_start = np.clip(cols - NA_KW // 2, 0, GRID_W - NA_KW)
    valid = (cols[None, :] >= col_start[:, None]) & (cols[None, :] < col_start[:, None] + NA_KW)
    dc = np.clip(cols[None, :] - cols[:, None], -(NA_KW - 1), NA_KW - 1) + (NA_KW - 1)
    onehot = (dc[None] == np.arange(2 * NA_KW - 1)[:, None, None]).astype(np.float32)
    expanded = jnp.einsum("hro,oqk->hrqk", rpb, onehot, precision=lax.Precision.HIGHEST)
    tiles = jnp.where(valid[None, None], expanded * LOG2_E, NEG_INF).astype(F32)
    two_rows = jnp.concatenate([tiles[:, :-1], tiles[:, 1:]], axis=-1)
    n_dr = two_rows.shape[1]
    stacked = two_rows.reshape(NA_HEADS // 2, 2, n_dr, GRID_W, LANES).transpose(0, 2, 1, 3, 4)
    return stacked.reshape(NA_HEADS // 2, n_dr, 2 * GRID_W, LANES)


def _na_meta_bias_pairs(meta_bias):
    return jnp.repeat(meta_bias.reshape(NA_HEADS // 2, 2, N_META) * LOG2_E, GRID_W, axis=1)


def kernel(x, meta_tokens, norm_mix_gain, w_in, rpb, meta_bias, w_gate_up_fwd, b_gate_fwd, w_gate_up_bwd, b_gate_bwd, gla_norm_gain, w_out, norm_ffn_gain, w_ffn_gate, w_ffn_up, w_ffn_down, norm_final_gain):
    bsz, seq, d = x.shape
    assert w_in.shape[0] == 1, "single-layer block: meta-token outputs are never consumed"
    assert w_in.shape[2] == IN_WIDTH and seq % GRID_W == 0 and GLA_CHUNK == GRID_W

    w_pad = jnp.concatenate([w_in[0], jnp.zeros((d, IN_WIDTH_PAD - IN_WIDTH), F32)], axis=1).astype(BF16)
    zeros_up = jnp.zeros((GATE_SLAB - 2 * GLA_GATE_RANK, GLA_KEY_WIDTH), F32)
    wupf = jnp.concatenate([w_gate_up_fwd[0], jnp.zeros_like(w_gate_up_bwd[0]), zeros_up], axis=0).astype(BF16)
    wupb = jnp.concatenate([jnp.zeros_like(w_gate_up_fwd[0]), w_gate_up_bwd[0], zeros_up], axis=0).astype(BF16)
    bgf = b_gate_fwd[0][None, :]
    bgb = b_gate_bwd[0][None, :]
    gain_mix = norm_mix_gain[0][None, :]

    x2d = x.reshape(bsz * seq, d)
    proj_args = (gain_mix, w_pad, wupf, bgf, wupb, bgb)
    naq, nak, nav, gq, gk, gv, gr, gf, gb = _in_proj(x2d, *proj_args, tm=512)
    _, km, vm, _, gk_m, gv_m, _, gf_m, _ = _in_proj(meta_tokens, *proj_args, tm=N_META)

    s0 = _gla_init(gk_m, gv_m, gf_m)

    per_batch = lambda a: a.reshape(bsz, seq, a.shape[-1])
    na_out = _na(per_batch(naq), per_batch(nak), per_batch(nav), km, vm,
                 _na_bias_pairs(rpb[0]), _na_meta_bias_pairs(meta_bias[0]))
    gla_out = _gla(per_batch(gq), per_batch(gk), per_batch(gv), per_batch(gf), per_batch(gb),
                   per_batch(gr), s0, gla_norm_gain[0][None, :])

    out = _out_ffn(x2d, na_out.reshape(bsz * seq, NA_WIDTH), gla_out.reshape(bsz * seq, GLA_VAL_WIDTH),
                   w_out[0].astype(BF16), norm_ffn_gain[0][None, :], w_ffn_gate[0].astype(BF16),
                   w_ffn_up[0].astype(BF16), w_ffn_down[0].astype(BF16), norm_final_gain[None, :], tm=512)
    return out.reshape(bsz, seq, d)
```

```python
import functools

import jax
import jax.numpy as jnp
import numpy as np
from jax import lax
from jax.experimental import pallas as pl
from jax.experimental.pallas import tpu as pltpu

F32 = jnp.float32
BF16 = jnp.bfloat16

N_META = 16
GRID_W = 64
NA_HEADS = 8
NA_HEAD_DIM = 64
NA_WIDTH = NA_HEADS * NA_HEAD_DIM
NA_KH_MAX = 8
NA_KW = 16
GLA_HEADS = 4
GLA_DK = 64
GLA_DV = 128
GLA_KEY_WIDTH = GLA_HEADS * GLA_DK
GLA_VAL_WIDTH = GLA_HEADS * GLA_DV
GLA_GATE_RANK = 16
GLA_GATE_TAU = 16.0
GLA_CHUNK = 64
RMS_EPS = 1e-6
NEG_INF = -1e30
LOG2_E = 1.4426950408889634
NA_Q_SCALE = NA_HEAD_DIM ** -0.5 * LOG2_E

LANES = 128
GATE_SLAB = LANES
VMEM_LIMIT = 48 * 1024 * 1024
MXU_TILE = 256
FFN_CHUNK = 4 * MXU_TILE
SUB_ROWS = 256

_OFF_NA_Q = 0
_OFF_NA_K = _OFF_NA_Q + NA_WIDTH
_OFF_NA_V = _OFF_NA_K + NA_WIDTH
_OFF_G_Q = _OFF_NA_V + NA_WIDTH
_OFF_G_K = _OFF_G_Q + GLA_KEY_WIDTH
_OFF_G_V = _OFF_G_K + GLA_KEY_WIDTH
_OFF_G_R = _OFF_G_V + GLA_VAL_WIDTH
_OFF_GATE = _OFF_G_R + GLA_VAL_WIDTH
IN_WIDTH = _OFF_GATE + 2 * GLA_GATE_RANK
IN_WIDTH_PAD = _OFF_GATE + GATE_SLAB


def _rms(x, gain):
    ms = jnp.mean(x * x, axis=-1, keepdims=True)
    return x * lax.rsqrt(ms + RMS_EPS) * gain


def _silu(x):
    return x * (1.0 / (1.0 + jnp.exp(-x)))


def _log_sigmoid(x):
    return jnp.minimum(x, 0.0) - jnp.log1p(jnp.exp(-jnp.abs(x)))


def _dot(a, b):
    return jnp.dot(a, b, preferred_element_type=F32)


def _dot_nt(a, b):
    return lax.dot_general(a, b, (((1,), (1,)), ((), ())), preferred_element_type=F32)


def _dot_tn(a, b):
    return lax.dot_general(a, b, (((0,), (0,)), ((), ())), preferred_element_type=F32)


def _head_masks(rows, dtype):
    lane = lax.broadcasted_iota(jnp.int32, (rows, LANES), 1)
    lo = jnp.where(lane < LANES // 2, 1.0, 0.0).astype(dtype)
    hi = jnp.where(lane >= LANES // 2, 1.0, 0.0).astype(dtype)
    return lo, hi


def _in_proj_kernel(x_ref, gain_ref, w_ref, wupf_ref, bf_ref, wupb_ref, bb_ref,
                    naq_ref, nak_ref, nav_ref, gq_ref, gk_ref, gv_ref, gr_ref, gf_ref, gb_ref):
    tm = x_ref.shape[0]
    sub = min(tm, 2 * SUB_ROWS)
    subs = [slice(r, r + sub) for r in range(0, tm, sub)]
    xns = [_rms(x_ref[rows, :], gain_ref[...]).astype(BF16) for rows in subs]
    for rows, xn in zip(subs, xns):

        def proj(lo, width):
            return _dot(xn, w_ref[:, lo:lo + width])

        low = proj(_OFF_GATE, GATE_SLAB).astype(BF16)
        gq_ref[rows, :] = proj(_OFF_G_Q, GLA_KEY_WIDTH) * (GLA_DK ** -0.5)
        gr_ref[rows, :] = _silu(proj(_OFF_G_R, GLA_VAL_WIDTH))
        pre_f = _dot(low, wupf_ref[...]) + bf_ref[...]
        pre_b = _dot(low, wupb_ref[...]) + bb_ref[...]
        gf_ref[rows, :] = _log_sigmoid(pre_f) * (1.0 / GLA_GATE_TAU)
        gb_ref[rows, :] = _log_sigmoid(pre_b) * (1.0 / GLA_GATE_TAU)
        gk_ref[rows, :] = proj(_OFF_G_K, GLA_KEY_WIDTH)
        naq_ref[rows, :] = (proj(_OFF_NA_Q, NA_WIDTH) * NA_Q_SCALE).astype(BF16)
        gv_ref[rows, :] = proj(_OFF_G_V, GLA_VAL_WIDTH).astype(BF16)
        nak_ref[rows, :] = proj(_OFF_NA_K, NA_WIDTH).astype(BF16)
        nav_ref[rows, :] = proj(_OFF_NA_V, NA_WIDTH).astype(BF16)


def _in_proj(x2d, gain, w_pad, wupf, bgf, wupb, bgb, tm):
    m, d = x2d.shape
    const = lambda shape: pl.BlockSpec(shape, lambda i: (0, 0))
    row = lambda width: pl.BlockSpec((tm, width), lambda i: (i, 0))
    widths =(NA_WIDTH, NA_WIDTH, NA_WIDTH, GLA_KEY_WIDTH, GLA_KEY_WIDTH,
              GLA_VAL_WIDTH, GLA_VAL_WIDTH, GLA_KEY_WIDTH, GLA_KEY_WIDTH)
    dtypes = (BF16, BF16, BF16, F32, F32, BF16, F32, F32, F32)
    return pl.pallas_call(
        _in_proj_kernel,
        grid=(m // tm,),
        in_specs=[row(d), const((1, d)), const(w_pad.shape), const(wupf.shape), const(bgf.shape),
                  const(wupb.shape), const(bgb.shape)],
        out_specs=[row(w) for w in widths],
        out_shape=[jax.ShapeDtypeStruct((m, w), dt) for w, dt in zip(widths, dtypes)],
        compiler_params=pltpu.CompilerParams(dimension_semantics=("arbitrary",),
                                             vmem_limit_bytes=VMEM_LIMIT),
        name="in_proj",
    )(x2d, gain, w_pad, wupf, bgf, wupb, bgb)


def _gla_init_kernel(gk_ref, gv_ref, gf_ref, st_ref):
    n = gk_ref.shape[0]
    r = lax.broadcasted_iota(jnp.int32, (n, n), 0)
    c = lax.broadcasted_iota(jnp.int32, (n, n), 1)
    later = jnp.where(c > r, 1.0, 0.0).astype(BF16)
    g = gf_ref[...]
    g_hi = g.astype(BF16)
    g_lo = (g - g_hi.astype(F32)).astype(BF16)
    tail = _dot(later, g_hi) + _dot(later, g_lo)
    k_d = (gk_ref[...] * jnp.exp(tail)).astype(BF16)
    for p in range(GLA_HEADS // 2):
        st_ref[p] = _dot_tn(k_d[:, LANES * p:LANES * (p + 1)], gv_ref[:, 2 * GLA_DV * p:2 * GLA_DV * (p + 1)])


def _gla_init(gk_m, gv_m, gf_m):
    return pl.pallas_call(
        _gla_init_kernel,
        out_shape=jax.ShapeDtypeStruct((GLA_HEADS // 2, LANES, 2 * GLA_DV), F32),
        name="gla_init",
    )(gk_m, gv_m, gf_m)


def _na_kernel(q_ref, k_ref, v_ref, km_ref, vm_ref, bias_ref, mb_ref, o_ref, sw_ref, sm_ref, *, n_rows):
    kh = min(NA_KH_MAX, n_rows)
    n_pairs = NA_HEADS // 2
    pair_lanes = [slice(LANES * p, LANES * (p + 1)) for p in range(n_pairs)]
    masks = _head_masks(GRID_W, BF16)
    sel_lo = lax.broadcasted_iota(jnp.int32, (GRID_W, LANES), 1) < LANES // 2

    def window_start(i):
        return jnp.clip(i - kh // 2, 0, n_rows - kh)

    def row_scores(i, slot):
        qoff = pl.multiple_of(i * GRID_W, GRID_W)
        koff = pl.multiple_of(window_start(i) * GRID_W, GRID_W)
        for p in range(n_pairs):
            q_pair = q_ref[0, pl.ds(qoff, GRID_W), pair_lanes[p]]
            q2 = jnp.concatenate([q_pair * masks[0], q_pair * masks[1]], axis=0)
            k_win = k_ref[0, pl.ds(koff, kh * GRID_W), pair_lanes[p]]
            sw_ref[slot, p] = _dot_nt(q2, k_win)
            sm_ref[slot, p] = _dot_nt(q2, km_ref[:, pair_lanes[p]])

    def row_finish(i, slot):
        s = window_start(i)
        dr0 = s - i + (NA_KH_MAX - 1)
        qoff = pl.multiple_of(i * GRID_W, GRID_W)
        koff = pl.multiple_of(s * GRID_W, GRID_W)
        biased = []
        for p in range(n_pairs):
            s_meta = sm_ref[slot, p] + mb_ref[p]
            chunks = [sw_ref[slot, p, :, LANES * t:LANES * (t + 1)] + bias_ref[p, dr0 + 2 * t]
                      for t in range(kh * GRID_W // LANES)]
            m_el = functools.reduce(jnp.maximum, chunks)
            m = jnp.maximum(jnp.max(m_el, axis=-1, keepdims=True),
                            jnp.max(s_meta, axis=-1, keepdims=True))
            biased.append((chunks, s_meta, m))
        probs = []
        for p in range(n_pairs):
            chunks, s_meta, m = biased[p]
            p_chunks = [jnp.exp2(c - m) for c in chunks]
            p_meta = jnp.exp2(s_meta - m)
            l = (jnp.sum(functools.reduce(jnp.add, p_chunks), axis=-1, keepdims=True)
                 + jnp.sum(p_meta, axis=-1, keepdims=True))
            probs.append((jnp.concatenate(p_chunks, axis=1).astype(BF16), p_meta.astype(BF16), l))
        pair_outs = []
        for p in range(n_pairs):
            p_win, p_meta, l = probs[p]
            v_win = v_ref[0, pl.ds(koff, kh * GRID_W), pair_lanes[p]]
            o2 = (_dot(p_win, v_win) + _dot(p_meta, vm_ref[:, pair_lanes[p]])) / l
            pair_outs.append(jnp.where(sel_lo, o2[:GRID_W], o2[GRID_W:]).astype(BF16))
        o_ref[0, pl.ds(qoff, GRID_W), :] = jnp.concatenate(pair_outs, axis=1)

    def two_rows(j, carry):
        a = 2 * j
        row_scores(a + 1, 1)
        row_finish(a, 0)
        row_scores(jnp.minimum(a + 2, n_rows - 1), 0)
        row_finish(a + 1, 1)
        return carry

    row_scores(0, 0)
    lax.fori_loop(0, n_rows // 2, two_rows, 0)


def _na(q, k, v, k_meta, v_meta, bias_pairs, meta_bias):
    b, n, w = q.shape
    n_rows = n // GRID_W
    assert n_rows % 2 == 0
    kh = min(NA_KH_MAX, n_rows)
    kern = functools.partial(_na_kernel, n_rows=n_rows)
    batch_spec = pl.BlockSpec((1, n, w), lambda bi: (bi, 0, 0))
    whole = lambda a: pl.BlockSpec(a.shape, lambda bi: (0,) * a.ndim)
    return pl.pallas_call(
        kern,
        grid=(b,),
        in_specs=[batch_spec, batch_spec, batch_spec, whole(k_meta), whole(v_meta), whole(bias_pairs),
                  whole(meta_bias)],
        out_specs=batch_spec,
        out_shape=jax.ShapeDtypeStruct((b, n, w), BF16),
        scratch_shapes=[pltpu.VMEM((2, NA_HEADS // 2, 2 * GRID_W, kh * GRID_W), F32),
                        pltpu.VMEM((2, NA_HEADS // 2, 2 * GRID_W, N_META), F32)],
        compiler_params=pltpu.CompilerParams(dimension_semantics=("arbitrary",),
                                             vmem_limit_bytes=VMEM_LIMIT),
        name="na",
    )(q, k, v, k_meta, v_meta, bias_pairs, meta_bias)


GLA_GROUP = 8
GLA_FINAL_ROWS = 256


def _gla_kernel(gq_ref, gk_ref, gv_ref, gf_ref, gb_ref, gr_ref, s0_ref, gain_ref, o_ref,
                ofwd_ref, obwd_ref, stf_ref, stb_ref):
    seq = gq_ref.shape[1]
    n_chunks = seq // GLA_CHUNK
    r = lax.broadcasted_iota(jnp.int32, (GLA_CHUNK, GLA_CHUNK), 0)
    c = lax.broadcasted_iota(jnp.int32, (GLA_CHUNK, GLA_CHUNK), 1)
    masks = _head_masks(GLA_CHUNK, F32)
    dirs = ((gf_ref, c <= r, GLA_CHUNK - 1, stf_ref, ofwd_ref),
            (gb_ref, c >= r, 0, stb_ref, obwd_ref))
    tris = [jnp.where(d[1], 1.0, 0.0).astype(BF16) for d in dirs]
    keep2 = [jnp.concatenate([d[1], d[1]], axis=0) for d in dirs]
    eye = (lax.broadcasted_iota(jnp.int32, (LANES, LANES), 0)
           == lax.broadcasted_iota(jnp.int32, (LANES, LANES), 1))

    stf_ref[...] = s0_ref[0]
    stb_ref[...] = jnp.zeros_like(stb_ref)

    def step(t, carry):
        chains = []
        for d in range(2):
            for j in range(GLA_GROUP):
                idx = t * GLA_GROUP + j
                n = idx if d == 0 else n_chunks - 1 - idx
                chains.append({"d": d, "rows": pl.ds(pl.multiple_of(n * GLA_CHUNK, GLA_CHUNK), GLA_CHUNK)})
        for ch in chains:
            g = dirs[ch["d"]][0][0, ch["rows"], :]
            g_hi = g.astype(BF16)
            g_lo = (g - g_hi.astype(F32)).astype(BF16)
            ch["b2"] = _dot(tris[ch["d"]], jnp.concatenate([g_hi, g_lo], axis=1))
        for ch in chains:
            last = dirs[ch["d"]][2]
            b = ch["b2"][:, :LANES] + ch["b2"][:, LANES:]
            b_last = b[last:last + 1, :]
            q_e = gq_ref[0, ch["rows"], :] * jnp.exp(b)
            k = gk_ref[0, ch["rows"], :]
            ch["qm2"] = jnp.concatenate([q_e * masks[0], q_e * masks[1]], axis=0).astype(BF16)
            ch["k_e"] = (k * jnp.exp(-b)).astype(BF16)
            ch["k_d"] = (k * jnp.exp(b_last - b)).astype(BF16)
            ch["decay"] = jnp.sum(jnp.where(eye, jnp.exp(b_last), 0.0), axis=1, keepdims=True)
        for ch in chains:
            ch["a2"] = _dot_nt(ch["qm2"], ch["k_e"])
            ch["kv"] = _dot_tn(ch["k_d"], gv_ref[0, ch["rows"], :])
        for d in range(2):
            st_ref = dirs[d][3]
            st = st_ref[...]
            for ch in chains:
                if ch["d"] == d:
                    ch["st_prev"] = st.astype(BF16)
                    st = ch["decay"] * st + ch["kv"]
            st_ref[...] = st
        for ch in chains:
            d = ch["d"]
            a2 = jnp.where(keep2[d], ch["a2"], 0.0).astype(BF16)
            o2 = _dot(a2, gv_ref[0, ch["rows"], :]) + _dot(ch["qm2"], ch["st_prev"])
            for hh in range(2):
                rows_h = slice(GLA_CHUNK * hh, GLA_CHUNK * (hh + 1))
                cols = slice(GLA_DV * hh, GLA_DV * (hh + 1))
                dirs[d][4][ch["rows"], cols] = o2[rows_h, cols]
        return carry

    lax.fori_loop(0, n_chunks // GLA_GROUP, step, 0)

    def finish(i, carry):
        rows = pl.ds(pl.multiple_of(i * GLA_FINAL_ROWS, GLA_FINAL_ROWS), GLA_FINAL_ROWS)
        for hh in range(2):
            cols = slice(GLA_DV * hh, GLA_DV * (hh + 1))
            o = _rms(ofwd_ref[rows, cols] + obwd_ref[rows, cols], gain_ref[...])
            o_ref[0, rows, cols] = (o * gr_ref[0, rows, cols]).astype(BF16)
        return carry

    lax.fori_loop(0, seq // GLA_FINAL_ROWS, finish, 0)


def _gla(gq, gk, gv, gf, gb, gr, s0, gain):
    b, n, _ = gq.shape
    assert (n // GLA_CHUNK) % GLA_GROUP == 0 and n % GLA_FINAL_ROWS == 0
    n_pairs = GLA_HEADS // 2
    key_spec = pl.BlockSpec((1, n, LANES), lambda bi, pi: (bi, 0, pi))
    val_spec = pl.BlockSpec((1, n, 2 * GLA_DV), lambda bi, pi: (bi, 0, pi))
    return pl.pallas_call(
        _gla_kernel,
        grid=(b, n_pairs),
        in_specs=[key_spec, key_spec, val_spec, key_spec, key_spec, val_spec,
                  pl.BlockSpec((1, LANES, 2 * GLA_DV), lambda bi, pi: (pi, 0, 0)),
                  pl.BlockSpec((1, GLA_DV), lambda bi, pi: (0, 0))],
        out_specs=val_spec,
        out_shape=jax.ShapeDtypeStruct((b, n, GLA_VAL_WIDTH), BF16),
        scratch_shapes=[pltpu.VMEM((n, 2 * GLA_DV), F32), pltpu.VMEM((n, 2 * GLA_DV), F32),
                        pltpu.VMEM((LANES, 2 * GLA_DV), F32), pltpu.VMEM((LANES, 2 * GLA_DV), F32)],
        compiler_params=pltpu.CompilerParams(dimension_semantics=("arbitrary", "arbitrary"),
                                             vmem_limit_bytes=VMEM_LIMIT),
        name="gla",
    )(gq, gk, gv, gf, gb, gr, s0, gain)


def _out_ffn_kernel(x_ref, na_ref, gla_ref, wout_ref, gffn_ref, wg_ref, wu_ref, wd_ref, gfin_ref, o_ref):
    subs = [slice(r, r + SUB_ROWS) for r in range(0, x_ref.shape[0], SUB_ROWS)]
    mixes = [_dot(na_ref[rows, :], wout_ref[0:NA_WIDTH, :]) + _dot(gla_ref[rows, :], wout_ref[NA_WIDTH:, :])
             for rows in subs]
    hs = [x_ref[rows, :] + mix for rows, mix in zip(subs, mixes)]
    hns = [_rms(h, gffn_ref[...]).astype(BF16) for h in hs]
    acts = [(_silu(_dot(hn, wg_ref[...])) * _dot(hn, wu_ref[...])).astype(BF16) for hn in hns]
    outs = [h + _dot(act, wd_ref[...]) for h, act in zip(hs, acts)]
    for rows, out in zip(subs, outs):
        o_ref[rows, :] = _rms(out, gfin_ref[...])


def _out_ffn(x2d, na2d, gla2d, w_out, g_ffn, w_gate, w_up, w_down, g_fin, tm):
    m, d = x2d.shape
    const = lambda shape: pl.BlockSpec(shape, lambda i: (0, 0))
    row = lambda width: pl.BlockSpec((tm, width), lambda i: (i, 0))
    return pl.pallas_call(
        _out_ffn_kernel,
        grid=(m // tm,),
        in_specs=[row(d), row(NA_WIDTH), row(GLA_VAL_WIDTH), const(w_out.shape), const((1, d)),
                  const(w_gate.shape), const(w_up.shape), const(w_down.shape), const((1, d))],
        out_specs=row(d),
        out_shape=jax.ShapeDtypeStruct((m, d), F32),
        compiler_params=pltpu.CompilerParams(dimension_semantics=("arbitrary",),
                                             vmem_limit_bytes=56 * 1024 * 1024),
        name="out_ffn",
    )(x2d, na2d, gla2d, w_out, g_ffn, w_gate, w_up, w_down, g_fin)


def _na_bias_pairs(rpb):
    cols = np.arange(GRID_W)
    col_start = np.clip(cols - NA_KW // 2, 0, GRID_W - NA_KW)
    valid = (cols[None, :] >= col_start[:, None]) & (cols[None, :] < col_start[:, None] + NA_KW)
    dc = np.clip(cols[None, :] - cols[:, None], -(NA_KW - 1), NA_KW - 1) + (NA_KW - 1)
    onehot = (dc[None] == np.arange(2 * NA_KW - 1)[:, None, None]).astype(np.float32)
    expanded = jnp.einsum("hro,oqk->hrqk", rpb, onehot, precision=lax.Precision.HIGHEST)
    tiles = jnp.where(valid[None, None], expanded * LOG2_E, NEG_INF).astype(F32)
    two_rows = jnp.concatenate([tiles[:, :-1], tiles[:, 1:]], axis=-1)
    n_dr = two_rows.shape[1]
    stacked = two_rows.reshape(NA_HEADS // 2, 2, n_dr, GRID_W, LANES).transpose(0, 2, 1, 3, 4)
    return stacked.reshape(NA_HEADS // 2, n_dr, 2 * GRID_W, LANES)


def _na_meta_bias_pairs(meta_bias):
    return jnp.repeat(meta_bias.reshape(NA_HEADS // 2, 2, N_META) * LOG2_E, GRID_W, axis=1)


def kernel(x, meta_tokens, norm_mix_gain, w_in, rpb, meta_bias, w_gate_up_fwd, b_gate_fwd, w_gate_up_bwd, b_gate_bwd, gla_norm_gain, w_out, norm_ffn_gain, w_ffn_gate, w_ffn_up, w_ffn_down, norm_final_gain):
    bsz, seq, d = x.shape
    assert w_in.shape[0] == 1, "single-layer block: meta-token outputs are never consumed"
    assert w_in.shape[2] == IN_WIDTH and seq % GRID_W == 0 and GLA_CHUNK == GRID_W

    w_pad = jnp.concatenate([w_in[0], jnp.zeros((d, IN_WIDTH_PAD - IN_WIDTH), F32)], axis=1).astype(BF16)
    zeros_up = jnp.zeros((GATE_SLAB - 2 * GLA_GATE_RANK, GLA_KEY_WIDTH), F32)
    wupf = jnp.concatenate([w_gate_up_fwd[0], jnp.zeros_like(w_gate_up_bwd[0]), zeros_up], axis=0).astype(BF16)
    wupb = jnp.concatenate([jnp.zeros_like(w_gate_up_fwd[0]), w_gate_up_bwd[0], zeros_up], axis=0).astype(BF16)
    bgf = b_gate_fwd[0][None, :]
    bgb = b_gate_bwd[0][None, :]
    gain_mix = norm_mix_gain[0][None, :]

    x2d = x.reshape(bsz * seq, d)
    proj_args = (gain_mix, w_pad, wupf, bgf, wupb, bgb)
    naq, nak, nav, gq, gk, gv, gr, gf, gb = _in_proj(x2d, *proj_args, tm=1024)
    _, km, vm, _, gk_m, gv_m, _, gf_m, _ = _in_proj(meta_tokens, *proj_args, tm=N_META)

    s0 = _gla_init(gk_m, gv_m, gf_m)

    per_batch = lambda a: a.reshape(bsz, seq, a.shape[-1])
    na_out = _na(per_batch(naq), per_batch(nak), per_batch(nav), km, vm,
                 _na_bias_pairs(rpb[0]), _na_meta_bias_pairs(meta_bias[0]))
    gla_out = _gla(per_batch(gq), per_batch(gk), per_batch(gv), per_batch(gf), per_batch(gb),
                   per_batch(gr), s0, gla_norm_gain[0][None, :])

    out = _out_ffn(x2d, na_out.reshape(bsz * seq, NA_WIDTH), gla_out.reshape(bsz * seq, GLA_VAL_WIDTH),
                   w_out[0].astype(BF16), norm_ffn_gain[0][None, :], w_ffn_gate[0].astype(BF16),
                   w_ffn_up[0].astype(BF16), w_ffn_down[0].astype(BF16), norm_final_gain[None, :], tm=512)
    return out.reshape(bsz, seq, d)
```

```python
import functools

import jax
import jax.numpy as jnp
import numpy as np
from jax import lax
from jax.experimental import pallas as pl
from jax.experimental.pallas import tpu as pltpu

F32 = jnp.float32
BF16 = jnp.bfloat16

N_META = 16
GRID_W = 64
NA_HEADS = 8
NA_HEAD_DIM = 64
NA_WIDTH = NA_HEADS * NA_HEAD_DIM
NA_KH_MAX = 8
NA_KW = 16
GLA_HEADS = 4
GLA_DK = 64
GLA_DV = 128
GLA_KEY_WIDTH = GLA_HEADS * GLA_DK
GLA_VAL_WIDTH = GLA_HEADS * GLA_DV
GLA_GATE_RANK = 16
GLA_GATE_TAU = 16.0
GLA_CHUNK = 64
RMS_EPS = 1e-6
NEG_INF = -1e30
LOG2_E = 1.4426950408889634
NA_Q_SCALE = NA_HEAD_DIM ** -0.5 * LOG2_E

LANES = 128
GATE_SLAB = LANES
VMEM_LIMIT = 48 * 1024 * 1024
MXU_TILE = 256
FFN_CHUNK = 4 * MXU_TILE
NA_ROWS_PER_STEP = 4
SUB_ROWS = 256

_OFF_NA_Q = 0
_OFF_NA_K = _OFF_NA_Q + NA_WIDTH
_OFF_NA_V = _OFF_NA_K + NA_WIDTH
_OFF_G_Q = _OFF_NA_V + NA_WIDTH
_OFF_G_K = _OFF_G_Q + GLA_KEY_WIDTH
_OFF_G_V = _OFF_G_K + GLA_KEY_WIDTH
_OFF_G_R = _OFF_G_V + GLA_VAL_WIDTH
_OFF_GATE = _OFF_G_R + GLA_VAL_WIDTH
IN_WIDTH = _OFF_GATE + 2 * GLA_GATE_RANK
IN_WIDTH_PAD = _OFF_GATE + GATE_SLAB


def _rms(x, gain):
    ms = jnp.mean(x * x, axis=-1, keepdims=True)
    return x * lax.rsqrt(ms + RMS_EPS) * gain


def _silu(x):
    return x * (1.0 / (1.0 + jnp.exp(-x)))


def _log_sigmoid(x):
    return jnp.minimum(x, 0.0) - jnp.log1p(jnp.exp(-jnp.abs(x)))


def _dot(a, b):
    return jnp.dot(a, b, preferred_element_type=F32)


def _dot_nt(a, b):
    return lax.dot_general(a, b, (((1,), (1,)), ((), ())), preferred_element_type=F32)


def _dot_tn(a, b):
    return lax.dot_general(a, b, (((0,), (0,)), ((), ())), preferred_element_type=F32)


def _head_masks(rows, dtype):
    lane = lax.broadcasted_iota(jnp.int32, (rows, LANES), 1)
    lo = jnp.where(lane < LANES // 2, 1.0, 0.0).astype(dtype)
    hi = jnp.where(lane >= LANES // 2, 1.0, 0.0).astype(dtype)
    return lo, hi


def _in_proj_kernel(x_ref, gain_ref, w_ref, wupf_ref, bf_ref, wupb_ref, bb_ref,
                    naq_ref, nak_ref, nav_ref, gq_ref, gk_ref, gv_ref, gr_ref, gf_ref, gb_ref):
    tm = x_ref.shape[0]
    sub = min(tm, 2 * SUB_ROWS)
    subs = [slice(r, r + sub) for r in range(0, tm, sub)]
    xns = [_rms(x_ref[rows, :], gain_ref[...]).astype(BF16) for rows in subs]
    for rows, xn in zip(subs, xns):

        def proj(lo, width):
            return _dot(xn, w_ref[:, lo:lo + width])

        low = proj(_OFF_GATE, GATE_SLAB).astype(BF16)
        gq_ref[rows, :] = proj(_OFF_G_Q, GLA_KEY_WIDTH) * (GLA_DK ** -0.5)
        gr_ref[rows, :] = _silu(proj(_OFF_G_R, GLA_VAL_WIDTH))
        pre_f = _dot(low, wupf_ref[...]) + bf_ref[...]
        pre_b = _dot(low, wupb_ref[...]) + bb_ref[...]
        gf_ref[rows, :] = _log_sigmoid(pre_f) * (1.0 / GLA_GATE_TAU)
        gb_ref[rows, :] = _log_sigmoid(pre_b) * (1.0 / GLA_GATE_TAU)
        gk_ref[rows, :] = proj(_OFF_G_K, GLA_KEY_WIDTH)
        naq_ref[rows, :] = (proj(_OFF_NA_Q, NA_WIDTH) * NA_Q_SCALE).astype(BF16)
        gv_ref[rows, :] = proj(_OFF_G_V, GLA_VAL_WIDTH).astype(BF16)
        nak_ref[rows, :] = proj(_OFF_NA_K, NA_WIDTH).astype(BF16)
        nav_ref[rows, :] = proj(_OFF_NA_V, NA_WIDTH).astype(BF16)


def _in_proj(x2d, gain, w_pad, wupf, bgf, wupb, bgb, tm):
    m, d = x2d.shape
    const = lambda shape: pl.BlockSpec(shape, lambda i: (0, 0))
    row = lambda width: pl.BlockSpec((tm, width), lambda i: (i, 0))
    widths =(NA_WIDTH, NA_WIDTH, NA_WIDTH, GLA_KEY_WIDTH, GLA_KEY_WIDTH,
              GLA_VAL_WIDTH, GLA_VAL_WIDTH, GLA_KEY_WIDTH, GLA_KEY_WIDTH)
    dtypes = (BF16, BF16, BF16, F32, F32, BF16, F32, F32, F32)
    return pl.pallas_call(
        _in_proj_kernel,
        grid=(m // tm,),
        in_specs=[row(d), const((1, d)), const(w_pad.shape), const(wupf.shape), const(bgf.shape),
                  const(wupb.shape), const(bgb.shape)],
        out_specs=[row(w) for w in widths],
        out_shape=[jax.ShapeDtypeStruct((m, w), dt) for w, dt in zip(widths, dtypes)],
        compiler_params=pltpu.CompilerParams(dimension_semantics=("arbitrary",),
                                             vmem_limit_bytes=VMEM_LIMIT),
        name="in_proj",
    )(x2d, gain, w_pad, wupf, bgf, wupb, bgb)


def _gla_init_kernel(gk_ref, gv_ref, gf_ref, st_ref):
    n = gk_ref.shape[0]
    r = lax.broadcasted_iota(jnp.int32, (n, n), 0)
    c = lax.broadcasted_iota(jnp.int32, (n, n), 1)
    later = jnp.where(c > r, 1.0, 0.0).astype(BF16)
    g = gf_ref[...]
    g_hi = g.astype(BF16)
    g_lo = (g - g_hi.astype(F32)).astype(BF16)
    tail = _dot(later, g_hi) + _dot(later, g_lo)
    k_d = (gk_ref[...] * jnp.exp(tail)).astype(BF16)
    for p in range(GLA_HEADS // 2):
        st_ref[p] = _dot_tn(k_d[:, LANES * p:LANES * (p + 1)], gv_ref[:, 2 * GLA_DV * p:2 * GLA_DV * (p + 1)])


def _gla_init(gk_m, gv_m, gf_m):
    return pl.pallas_call(
        _gla_init_kernel,
        out_shape=jax.ShapeDtypeStruct((GLA_HEADS // 2, LANES, 2 * GLA_DV), F32),
        name="gla_init",
    )(gk_m, gv_m, gf_m)


def _na_kernel(q_ref, k_ref, v_ref, km_ref, vm_ref, bias_ref, mb_ref, o_ref, sw_ref, sm_ref, *, n_rows):
    kh = min(NA_KH_MAX, n_rows)
    n_pairs = NA_HEADS // 2
    pair_lanes = [slice(LANES * p, LANES * (p + 1)) for p in range(n_pairs)]
    masks = _head_masks(GRID_W, BF16)
    sel_lo = lax.broadcasted_iota(jnp.int32, (GRID_W, LANES), 1) < LANES // 2

    def window_start(i):
        return jnp.clip(i - kh // 2, 0, n_rows - kh)

    def row_scores(i, slot):
        qoff = pl.multiple_of(i * GRID_W, GRID_W)
        koff = pl.multiple_of(window_start(i) * GRID_W, GRID_W)
        for p in range(n_pairs):
            q_pair = q_ref[0, pl.ds(qoff, GRID_W), pair_lanes[p]]
            q2 = jnp.concatenate([q_pair * masks[0], q_pair * masks[1]], axis=0)
            k_win = k_ref[0, pl.ds(koff, kh * GRID_W), pair_lanes[p]]
            sw_ref[slot, p] = _dot_nt(q2, k_win)
            sm_ref[slot, p] = _dot_nt(q2, km_ref[:, pair_lanes[p]])

    def row_finish(i, slot):
        s = window_start(i)
        dr0 = s - i + (NA_KH_MAX - 1)
        qoff = pl.multiple_of(i * GRID_W, GRID_W)
        koff = pl.multiple_of(s * GRID_W, GRID_W)
        biased = []
        for p in range(n_pairs):
            s_meta = sm_ref[slot, p] + mb_ref[p]
            chunks = [sw_ref[slot, p, :, LANES * t:LANES * (t + 1)] + bias_ref[p, dr0 + 2 * t]
                      for t in range(kh * GRID_W // LANES)]
            m_el = functools.reduce(jnp.maximum, chunks)
            m = jnp.maximum(jnp.max(m_el, axis=-1, keepdims=True),
                            jnp.max(s_meta, axis=-1, keepdims=True))
            biased.append((chunks, s_meta, m))
        probs = []
        for p in range(n_pairs):
            chunks, s_meta, m = biased[p]
            p_chunks = [jnp.exp2(c - m) for c in chunks]
            p_meta = jnp.exp2(s_meta - m)
            l = (jnp.sum(functools.reduce(jnp.add, p_chunks), axis=-1, keepdims=True)
                 + jnp.sum(p_meta, axis=-1, keepdims=True))
            probs.append((jnp.concatenate(p_chunks, axis=1).astype(BF16), p_meta.astype(BF16), l))
        pair_outs = []
        for p in range(n_pairs):
            p_win, p_meta, l = probs[p]
            v_win = v_ref[0, pl.ds(koff, kh * GRID_W), pair_lanes[p]]
            o2 = (_dot(p_win, v_win) + _dot(p_meta, vm_ref[:, pair_lanes[p]])) / l
            pair_outs.append(jnp.where(sel_lo, o2[:GRID_W], o2[GRID_W:]).astype(BF16))
        o_ref[0, pl.ds(qoff, GRID_W), :] = jnp.concatenate(pair_outs, axis=1)

    def rows_step(j, carry):
        a = NA_ROWS_PER_STEP * j
        for u in range(NA_ROWS_PER_STEP):
            row_scores(jnp.minimum(a + u + 1, n_rows - 1), (u + 1) % 2)
            row_finish(a + u, u % 2)
        return carry

    row_scores(0, 0)
    lax.fori_loop(0, n_rows // NA_ROWS_PER_STEP, rows_step, 0)


def _na(q, k, v, k_meta, v_meta, bias_pairs, meta_bias):
    b, n, w = q.shape
    n_rows = n // GRID_W
    assert n_rows % NA_ROWS_PER_STEP == 0
    kh = min(NA_KH_MAX, n_rows)
    kern = functools.partial(_na_kernel, n_rows=n_rows)
    batch_spec = pl.BlockSpec((1, n, w), lambda bi: (bi, 0, 0))
    whole = lambda a: pl.BlockSpec(a.shape, lambda bi: (0,) * a.ndim)
    return pl.pallas_call(
        kern,
        grid=(b,),
        in_specs=[batch_spec, batch_spec, batch_spec, whole(k_meta), whole(v_meta), whole(bias_pairs),
                  whole(meta_bias)],
        out_specs=batch_spec,
        out_shape=jax.ShapeDtypeStruct((b, n, w), BF16),
        scratch_shapes=[pltpu.VMEM((2, NA_HEADS // 2, 2 * GRID_W, kh * GRID_W), F32),
                        pltpu.VMEM((2, NA_HEADS // 2, 2 * GRID_W, N_META), F32)],
        compiler_params=pltpu.CompilerParams(dimension_semantics=("arbitrary",),
                                             vmem_limit_bytes=VMEM_LIMIT),
        name="na",
    )(q, k, v, k_meta, v_meta, bias_pairs, meta_bias)


GLA_GROUP = 8
GLA_FINAL_ROWS = 1024


def _gla_kernel(gq_ref, gk_ref, gv_ref, gf_ref, gb_ref, gr_ref, s0_ref, gain_ref, o_ref,
                ofwd_ref, obwd_ref, stf_ref, stb_ref):
    seq = gq_ref.shape[1]
    n_chunks = seq // GLA_CHUNK
    r = lax.broadcasted_iota(jnp.int32, (GLA_CHUNK, GLA_CHUNK), 0)
    c = lax.broadcasted_iota(jnp.int32, (GLA_CHUNK, GLA_CHUNK), 1)
    masks = _head_masks(GLA_CHUNK, F32)
    dirs = ((gf_ref, c <= r, GLA_CHUNK - 1, stf_ref, ofwd_ref),
            (gb_ref, c >= r, 0, stb_ref, obwd_ref))
    tris = [jnp.where(d[1], 1.0, 0.0).astype(BF16) for d in dirs]
    keep2 = [jnp.concatenate([d[1], d[1]], axis=0) for d in dirs]
    eye = (lax.broadcasted_iota(jnp.int32, (LANES, LANES), 0)
           == lax.broadcasted_iota(jnp.int32, (LANES, LANES), 1))

    stf_ref[...] = s0_ref[0]
    stb_ref[...] = jnp.zeros_like(stb_ref)

    def step(t, carry):
        chains = []
        for d in range(2):
            for j in range(GLA_GROUP):
                idx = t * GLA_GROUP + j
                n = idx if d == 0 else n_chunks - 1 - idx
                chains.append({"d": d, "rows": pl.ds(pl.multiple_of(n * GLA_CHUNK, GLA_CHUNK), GLA_CHUNK)})
        for ch in chains:
            g = dirs[ch["d"]][0][0, ch["rows"], :]
            g_hi = g.astype(BF16)
            g_lo = (g - g_hi.astype(F32)).astype(BF16)
            ch["b2"] = _dot(tris[ch["d"]], jnp.concatenate([g_hi, g_lo], axis=1))
        for ch in chains:
            last = dirs[ch["d"]][2]
            b = ch["b2"][:, :LANES] + ch["b2"][:, LANES:]
            b_last = b[last:last + 1, :]
            q_e = gq_ref[0, ch["rows"], :] * jnp.exp(b)
            k = gk_ref[0, ch["rows"], :]
            ch["qm2"] = jnp.concatenate([q_e * masks[0], q_e * masks[1]], axis=0).astype(BF16)
            ch["k_e"] = (k * jnp.exp(-b)).astype(BF16)
            ch["k_d"] = (k * jnp.exp(b_last - b)).astype(BF16)
            ch["decay"] = jnp.sum(jnp.where(eye, jnp.exp(b_last), 0.0), axis=1, keepdims=True)
        for ch in chains:
            ch["a2"] = _dot_nt(ch["qm2"], ch["k_e"])
            ch["kv"] = _dot_tn(ch["k_d"], gv_ref[0, ch["rows"], :])
        for d in range(2):
            st_ref = dirs[d][3]
            st = st_ref[...]
            for ch in chains:
                if ch["d"] == d:
                    ch["st_prev"] = st.astype(BF16)
                    st = ch["decay"] * st + ch["kv"]
            st_ref[...] = st
        for ch in chains:
            d = ch["d"]
            a2 = jnp.where(keep2[d], ch["a2"], 0.0).astype(BF16)
            o2 = _dot(a2, gv_ref[0, ch["rows"], :]) + _dot(ch["qm2"], ch["st_prev"])
            for hh in range(2):
                rows_h = slice(GLA_CHUNK * hh, GLA_CHUNK * (hh + 1))
                cols = slice(GLA_DV * hh, GLA_DV * (hh + 1))
                dirs[d][4][ch["rows"], cols] = o2[rows_h, cols]
        return carry

    lax.fori_loop(0, n_chunks // GLA_GROUP, step, 0)

    def finish(i, carry):
        rows = pl.ds(pl.multiple_of(i * GLA_FINAL_ROWS, GLA_FINAL_ROWS), GLA_FINAL_ROWS)
        for hh in range(2):
            cols = slice(GLA_DV * hh, GLA_DV * (hh + 1))
            o = _rms(ofwd_ref[rows, cols] + obwd_ref[rows, cols], gain_ref[...])
            o_ref[0, rows, cols] = (o * gr_ref[0, rows, cols]).astype(BF16)
        return carry

    lax.fori_loop(0, seq // GLA_FINAL_ROWS, finish, 0)


def _gla(gq, gk, gv, gf, gb, gr, s0, gain):
    b, n, _ = gq.shape
    assert (n // GLA_CHUNK) % GLA_GROUP == 0 and n % GLA_FINAL_ROWS == 0
    n_pairs = GLA_HEADS // 2
    key_spec = pl.BlockSpec((1, n, LANES), lambda bi, pi: (bi, 0, pi))
    val_spec = pl.BlockSpec((1, n, 2 * GLA_DV), lambda bi, pi: (bi, 0, pi))
    return pl.pallas_call(
        _gla_kernel,
        grid=(b, n_pairs),
        in_specs=[key_spec, key_spec, val_spec, key_spec, key_spec, val_spec,
                  pl.BlockSpec((1, LANES, 2 * GLA_DV), lambda bi, pi: (pi, 0, 0)),
                  pl.BlockSpec((1, GLA_DV), lambda bi, pi: (0, 0))],
        out_specs=val_spec,
        out_shape=jax.ShapeDtypeStruct((b, n, GLA_VAL_WIDTH), BF16),
        scratch_shapes=[pltpu.VMEM((n, 2 * GLA_DV), F32), pltpu.VMEM((n, 2 * GLA_DV), F32),
                        pltpu.VMEM((LANES, 2 * GLA_DV), F32), pltpu.VMEM((LANES, 2 * GLA_DV), F32)],
        compiler_params=pltpu.CompilerParams(dimension_semantics=("arbitrary", "arbitrary"),
                                             vmem_limit_bytes=VMEM_LIMIT),
        name="gla",
    )(gq, gk, gv, gf, gb, gr, s0, gain)


def _out_ffn_kernel(x_ref, na_ref, gla_ref, wout_ref, gffn_ref, wg_ref, wu_ref, wd_ref, gfin_ref, o_ref):
    subs = [slice(r, r + SUB_ROWS) for r in range(0, x_ref.shape[0], SUB_ROWS)]
    mixes = [_dot(na_ref[rows, :], wout_ref[0:NA_WIDTH, :]) + _dot(gla_ref[rows, :], wout_ref[NA_WIDTH:, :])
             for rows in subs]
    hs = [x_ref[rows, :] + mix for rows, mix in zip(subs, mixes)]
    hns = [_rms(h, gffn_ref[...]).astype(BF16) for h in hs]
    acts = [(_silu(_dot(hn, wg_ref[...])) * _dot(hn, wu_ref[...])).astype(BF16) for hn in hns]
    outs = [h + _dot(act, wd_ref[...]) for h, act in zip(hs, acts)]
    for rows, out in zip(subs, outs):
        o_ref[rows, :] = _rms(out, gfin_ref[...])


def _out_ffn(x2d, na2d, gla2d, w_out, g_ffn, w_gate, w_up, w_down, g_fin, tm):
    m, d = x2d.shape
    const = lambda shape: pl.BlockSpec(shape, lambda i: (0, 0))
    row = lambda width: pl.BlockSpec((tm, width), lambda i: (i, 0))
    return pl.pallas_call(
        _out_ffn_kernel,
        grid=(m // tm,),
        in_specs=[row(d), row(NA_WIDTH), row(GLA_VAL_WIDTH), const(w_out.shape), const((1, d)),
                  const(w_gate.shape), const(w_up.shape), const(w_down.shape), const((1, d))],
        out_specs=row(d),
        out_shape=jax.ShapeDtypeStruct((m, d), F32),
        compiler_params=pltpu.CompilerParams(dimension_semantics=("arbitrary",),
                                             vmem_limit_bytes=56 * 1024 * 1024),
        name="out_ffn",
    )(x2d, na2d, gla2d, w_out, g_ffn, w_gate, w_up, w_down, g_fin)


def _na_bias_pairs(rpb):
    cols = np.arange(GRID_W)
    col_start = np.clip(cols - NA_KW // 2, 0, GRID_W - NA_KW)
    valid = (cols[None, :] >= col_start[:, None]) & (cols[None, :] < col_start[:, None] + NA_KW)
    dc = np.clip(cols[None, :] - cols[:, None], -(NA_KW - 1), NA_KW - 1) + (NA_KW - 1)
    onehot = (dc[None] == np.arange(2 * NA_KW - 1)[:, None, None]).astype(np.float32)
    expanded = jnp.einsum("hro,oqk->hrqk", rpb, onehot, precision=lax.Precision.HIGHEST)
    tiles = jnp.where(valid[None, None], expanded * LOG2_E, NEG_INF).astype(F32)
    two_rows = jnp.concatenate([tiles[:, :-1], tiles[:, 1:]], axis=-1)
    n_dr = two_rows.shape[1]
    stacked = two_rows.reshape(NA_HEADS // 2, 2, n_dr, GRID_W, LANES).transpose(0, 2, 1, 3, 4)
    return stacked.reshape(NA_HEADS // 2, n_dr, 2 * GRID_W, LANES)


def _na_meta_bias_pairs(meta_bias):
    return jnp.repeat(meta_bias.reshape(NA_HEADS // 2, 2, N_META) * LOG2_E, GRID_W, axis=1)


def kernel(x, meta_tokens, norm_mix_gain, w_in, rpb, meta_bias, w_gate_up_fwd, b_gate_fwd, w_gate_up_bwd, b_gate_bwd, gla_norm_gain, w_out, norm_ffn_gain, w_ffn_gate, w_ffn_up, w_ffn_down, norm_final_gain):
    bsz, seq, d = x.shape
    assert w_in.shape[0] == 1, "single-layer block: meta-token outputs are never consumed"
    assert w_in.shape[2] == IN_WIDTH and seq % GRID_W == 0 and GLA_CHUNK == GRID_W

    w_pad = jnp.concatenate([w_in[0], jnp.zeros((d, IN_WIDTH_PAD - IN_WIDTH), F32)], axis=1).astype(BF16)
    zeros_up = jnp.zeros((GATE_SLAB - 2 * GLA_GATE_RANK, GLA_KEY_WIDTH), F32)
    wupf = jnp.concatenate([w_gate_up_fwd[0], jnp.zeros_like(w_gate_up_bwd[0]), zeros_up], axis=0).astype(BF16)
    wupb = jnp.concatenate([jnp.zeros_like(w_gate_up_fwd[0]), w_gate_up_bwd[0], zeros_up], axis=0).astype(BF16)
    bgf = b_gate_fwd[0][None, :]
    bgb = b_gate_bwd[0][None, :]
    gain_mix = norm_mix_gain[0][None, :]

    x2d = x.reshape(bsz * seq, d)
    proj_args = (gain_mix, w_pad, wupf, bgf, wupb, bgb)
    naq, nak, nav, gq, gk, gv, gr, gf, gb = _in_proj(x2d, *proj_args, tm=1024)
    _, km, vm, _, gk_m, gv_m, _, gf_m, _ = _in_proj(meta_tokens, *proj_args, tm=N_META)

    s0 = _gla_init(gk_m, gv_m, gf_m)

    per_batch = lambda a: a.reshape(bsz, seq, a.shape[-1])
    na_out = _na(per_batch(naq), per_batch(nak), per_batch(nav), km, vm,
                 _na_bias_pairs(rpb[0]), _na_meta_bias_pairs(meta_bias[0]))
    gla_out = _gla(per_batch(gq), per_batch(gk), per_batch(gv), per_batch(gf), per_batch(gb),
                   per_batch(gr), s0, gla_norm_gain[0][None, :])

    out = _out_ffn(x2d, na_out.reshape(bsz * seq, NA_WIDTH), gla_out.reshape(bsz * seq, GLA_VAL_WIDTH),
                   w_out[0].astype(BF16), norm_ffn_gain[0][None, :], w_ffn_gate[0].astype(BF16),
                   w_ffn_up[0].astype(BF16), w_ffn_down[0].astype(BF16), norm_final_gain[None, :], tm=512)
    return out.reshape(bsz, seq, d)
```

```python
import functools

import jax
import jax.numpy as jnp
import numpy as np
from jax import lax
from jax.experimental import pallas as pl
from jax.experimental.pallas import tpu as pltpu

F32 = jnp.float32
BF16 = jnp.bfloat16

N_META = 16
GRID_W = 64
NA_HEADS = 8
NA_HEAD_DIM = 64
NA_WIDTH = NA_HEADS * NA_HEAD_DIM
NA_KH_MAX = 8
NA_KW = 16
GLA_HEADS = 4
GLA_DK = 64
GLA_DV = 128
GLA_KEY_WIDTH = GLA_HEADS * GLA_DK
GLA_VAL_WIDTH = GLA_HEADS * GLA_DV
GLA_GATE_RANK = 16
GLA_GATE_TAU = 16.0
GLA_CHUNK = 64
RMS_EPS = 1e-6
NEG_INF = -1e30
LOG2_E = 1.4426950408889634
NA_Q_SCALE = NA_HEAD_DIM ** -0.5 * LOG2_E

LANES = 128
GATE_SLAB = LANES
VMEM_LIMIT = 48 * 1024 * 1024
MXU_TILE = 256
FFN_CHUNK = 4 * MXU_TILE
NA_ROWS_PER_STEP = 4
SUB_ROWS = 256

_OFF_NA_Q = 0
_OFF_NA_K = _OFF_NA_Q + NA_WIDTH
_OFF_NA_V = _OFF_NA_K + NA_WIDTH
_OFF_G_Q = _OFF_NA_V + NA_WIDTH
_OFF_G_K = _OFF_G_Q + GLA_KEY_WIDTH
_OFF_G_V = _OFF_G_K + GLA_KEY_WIDTH
_OFF_G_R = _OFF_G_V + GLA_VAL_WIDTH
_OFF_GATE = _OFF_G_R + GLA_VAL_WIDTH
IN_WIDTH = _OFF_GATE + 2 * GLA_GATE_RANK
IN_WIDTH_PAD = _OFF_GATE + GATE_SLAB


def _rms(x, gain):
    ms = jnp.mean(x * x, axis=-1, keepdims=True)
    return x * lax.rsqrt(ms + RMS_EPS) * gain


def _silu(x):
    return x * (1.0 / (1.0 + jnp.exp(-x)))


def _log_sigmoid(x):
    return jnp.minimum(x, 0.0) - jnp.log1p(jnp.exp(-jnp.abs(x)))


def _dot(a, b):
    return jnp.dot(a, b, preferred_element_type=F32)


def _dot_nt(a, b):
    return lax.dot_general(a, b, (((1,), (1,)), ((), ())), preferred_element_type=F32)


def _dot_tn(a, b):
    return lax.dot_general(a, b, (((0,), (0,)), ((), ())), preferred_element_type=F32)


def _head_masks(rows, dtype):
    lane = lax.broadcasted_iota(jnp.int32, (rows, LANES), 1)
    lo = jnp.where(lane < LANES // 2, 1.0, 0.0).astype(dtype)
    hi = jnp.where(lane >= LANES // 2, 1.0, 0.0).astype(dtype)
    return lo, hi


def _in_proj_kernel(x_ref, gain_ref, w_ref, wupf_ref, bf_ref, wupb_ref, bb_ref,
                    naq_ref, nak_ref, nav_ref, gq_ref, gk_ref, gv_ref, gr_ref, gf_ref, gb_ref):
    tm = x_ref.shape[0]
    sub = min(tm, 2 * SUB_ROWS)
    subs = [slice(r, r + sub) for r in range(0, tm, sub)]
    xns = [_rms(x_ref[rows, :], gain_ref[...]).astype(BF16) for rows in subs]
    for rows, xn in zip(subs, xns):

        def proj(lo, width):
            return _dot(xn, w_ref[:, lo:lo + width])

        half = GLA_VAL_WIDTH // 2
        low = proj(_OFF_GATE, GATE_SLAB).astype(BF16)
        naq_ref[rows, :] = (proj(_OFF_NA_Q, NA_WIDTH) * NA_Q_SCALE).astype(BF16)
        gr_ref[rows, 0:half] = _silu(proj(_OFF_G_R, half))
        gq_ref[rows, :] = proj(_OFF_G_Q, GLA_KEY_WIDTH) * (GLA_DK ** -0.5)
        gf_ref[rows, :] = _log_sigmoid(_dot(low, wupf_ref[...]) + bf_ref[...]) * (1.0 / GLA_GATE_TAU)
        gv_ref[rows, :] = proj(_OFF_G_V, GLA_VAL_WIDTH).astype(BF16)
        gr_ref[rows, half:] = _silu(proj(_OFF_G_R + half, half))
        gk_ref[rows, :] = proj(_OFF_G_K, GLA_KEY_WIDTH)
        gb_ref[rows, :] = _log_sigmoid(_dot(low, wupb_ref[...]) + bb_ref[...]) * (1.0 / GLA_GATE_TAU)
        nak_ref[rows, :] = proj(_OFF_NA_K, NA_WIDTH).astype(BF16)
        nav_ref[rows, :] = proj(_OFF_NA_V, NA_WIDTH).astype(BF16)


def _in_proj(x2d, gain, w_pad, wupf, bgf, wupb, bgb, tm):
    m, d = x2d.shape
    const = lambda shape: pl.BlockSpec(shape, lambda i: (0, 0))
    row = lambda width: pl.BlockSpec((tm, width), lambda i: (i, 0))
    widths =(NA_WIDTH, NA_WIDTH, NA_WIDTH, GLA_KEY_WIDTH, GLA_KEY_WIDTH,
              GLA_VAL_WIDTH, GLA_VAL_WIDTH, GLA_KEY_WIDTH, GLA_KEY_WIDTH)
    dtypes = (BF16, BF16, BF16, F32, F32, BF16, F32, F32, F32)
    return pl.pallas_call(
        _in_proj_kernel,
        grid=(m // tm,),
        in_specs=[row(d), const((1, d)), const(w_pad.shape), const(wupf.shape), const(bgf.shape),
                  const(wupb.shape), const(bgb.shape)],
        out_specs=[row(w) for w in widths],
        out_shape=[jax.ShapeDtypeStruct((m, w), dt) for w, dt in zip(widths, dtypes)],
        compiler_params=pltpu.CompilerParams(dimension_semantics=("arbitrary",),
                                             vmem_limit_bytes=VMEM_LIMIT),
        name="in_proj",
    )(x2d, gain, w_pad, wupf, bgf, wupb, bgb)


def _gla_init_kernel(gk_ref, gv_ref, gf_ref, st_ref):
    n = gk_ref.shape[0]
    r = lax.broadcasted_iota(jnp.int32, (n, n), 0)
    c = lax.broadcasted_iota(jnp.int32, (n, n), 1)
    later = jnp.where(c > r, 1.0, 0.0).astype(BF16)
    g = gf_ref[...]
    g_hi = g.astype(BF16)
    g_lo = (g - g_hi.astype(F32)).astype(BF16)
    tail = _dot(later, g_hi) + _dot(later, g_lo)
    k_d = (gk_ref[...] * jnp.exp(tail)).astype(BF16)
    for p in range(GLA_HEADS // 2):
        st_ref[p] = _dot_tn(k_d[:, LANES * p:LANES * (p + 1)], gv_ref[:, 2 * GLA_DV * p:2 * GLA_DV * (p + 1)])


def _gla_init(gk_m, gv_m, gf_m):
    return pl.pallas_call(
        _gla_init_kernel,
        out_shape=jax.ShapeDtypeStruct((GLA_HEADS // 2, LANES, 2 * GLA_DV), F32),
        name="gla_init",
    )(gk_m, gv_m, gf_m)


def _na_kernel(q_ref, k_ref, v_ref, km_ref, vm_ref, bias_ref, mb_ref, o_ref, sw_ref, sm_ref, *, n_rows):
    kh = min(NA_KH_MAX, n_rows)
    n_pairs = NA_HEADS // 2
    pair_lanes = [slice(LANES * p, LANES * (p + 1)) for p in range(n_pairs)]
    masks = _head_masks(GRID_W, BF16)
    sel_lo = lax.broadcasted_iota(jnp.int32, (GRID_W, LANES), 1) < LANES // 2

    def window_start(i):
        return jnp.clip(i - kh // 2, 0, n_rows - kh)

    def row_scores(i, slot):
        qoff = pl.multiple_of(i * GRID_W, GRID_W)
        koff = pl.multiple_of(window_start(i) * GRID_W, GRID_W)
        for p in range(n_pairs):
            q_pair = q_ref[0, pl.ds(qoff, GRID_W), pair_lanes[p]]
            q2 = jnp.concatenate([q_pair * masks[0], q_pair * masks[1]], axis=0)
            k_win = k_ref[0, pl.ds(koff, kh * GRID_W), pair_lanes[p]]
            sw_ref[slot, p] = _dot_nt(q2, k_win)
            sm_ref[slot, p] = _dot_nt(q2, km_ref[:, pair_lanes[p]])

    def row_finish(i, slot):
        s = window_start(i)
        dr0 = s - i + (NA_KH_MAX - 1)
        qoff = pl.multiple_of(i * GRID_W, GRID_W)
        koff = pl.multiple_of(s * GRID_W, GRID_W)
        biased = []
        for p in range(n_pairs):
            s_meta = sm_ref[slot, p] + mb_ref[p]
            chunks = [sw_ref[slot, p, :, LANES * t:LANES * (t + 1)] + bias_ref[p, dr0 + 2 * t]
                      for t in range(kh * GRID_W // LANES)]
            m_el = functools.reduce(jnp.maximum, chunks)
            m = jnp.maximum(jnp.max(m_el, axis=-1, keepdims=True),
                            jnp.max(s_meta, axis=-1, keepdims=True))
            biased.append((chunks, s_meta, m))
        probs = []
        for p in range(n_pairs):
            chunks, s_meta, m = biased[p]
            p_chunks = [jnp.exp2(c - m) for c in chunks]
            p_meta = jnp.exp2(s_meta - m)
            l = (jnp.sum(functools.reduce(jnp.add, p_chunks), axis=-1, keepdims=True)
                 + jnp.sum(p_meta, axis=-1, keepdims=True))
            probs.append((jnp.concatenate(p_chunks, axis=1).astype(BF16), p_meta.astype(BF16), l))
        pair_outs = []
        for p in range(n_pairs):
            p_win, p_meta, l = probs[p]
            v_win = v_ref[0, pl.ds(koff, kh * GRID_W), pair_lanes[p]]
            o2 = (_dot(p_win, v_win) + _dot(p_meta, vm_ref[:, pair_lanes[p]])) / l
            pair_outs.append(jnp.where(sel_lo, o2[:GRID_W], o2[GRID_W:]).astype(BF16))
        o_ref[0, pl.ds(qoff, GRID_W), :] = jnp.concatenate(pair_outs, axis=1)

    def rows_step(j, carry):
        a = NA_ROWS_PER_STEP * j
        for u in range(NA_ROWS_PER_STEP):
            row_scores(jnp.minimum(a + u + 1, n_rows - 1), (u + 1) % 2)
            row_finish(a + u, u % 2)
        return carry

    row_scores(0, 0)
    lax.fori_loop(0, n_rows // NA_ROWS_PER_STEP, rows_step, 0)


def _na(q, k, v, k_meta, v_meta, bias_pairs, meta_bias):
    b, n, w = q.shape
    n_rows = n // GRID_W
    assert n_rows % NA_ROWS_PER_STEP == 0
    kh = min(NA_KH_MAX, n_rows)
    kern = functools.partial(_na_kernel, n_rows=n_rows)
    batch_spec = pl.BlockSpec((1, n, w), lambda bi: (bi, 0, 0))
    whole = lambda a: pl.BlockSpec(a.shape, lambda bi: (0,) * a.ndim)
    return pl.pallas_call(
        kern,
        grid=(b,),
        in_specs=[batch_spec, batch_spec, batch_spec, whole(k_meta), whole(v_meta), whole(bias_pairs),
                  whole(meta_bias)],
        out_specs=batch_spec,
        out_shape=jax.ShapeDtypeStruct((b, n, w), BF16),
        scratch_shapes=[pltpu.VMEM((2, NA_HEADS // 2, 2 * GRID_W, kh * GRID_W), F32),
                        pltpu.VMEM((2, NA_HEADS // 2, 2 * GRID_W, N_META), F32)],
        compiler_params=pltpu.CompilerParams(dimension_semantics=("arbitrary",),
                                             vmem_limit_bytes=VMEM_LIMIT),
        name="na",
    )(q, k, v, k_meta, v_meta, bias_pairs, meta_bias)


GLA_GROUP = 8
GLA_FINAL_ROWS = 1024


def _gla_kernel(gq_ref, gk_ref, gv_ref, gf_ref, gb_ref, gr_ref, s0_ref, gain_ref, o_ref,
                ofwd_ref, obwd_ref, stf_ref, stb_ref):
    seq = gq_ref.shape[1]
    n_chunks = seq // GLA_CHUNK
    r = lax.broadcasted_iota(jnp.int32, (GLA_CHUNK, GLA_CHUNK), 0)
    c = lax.broadcasted_iota(jnp.int32, (GLA_CHUNK, GLA_CHUNK), 1)
    masks = _head_masks(GLA_CHUNK, F32)
    dirs = ((gf_ref, c <= r, GLA_CHUNK - 1, stf_ref, ofwd_ref),
            (gb_ref, c >= r, 0, stb_ref, obwd_ref))
    tris = [jnp.where(d[1], 1.0, 0.0).astype(BF16) for d in dirs]
    keep2 = [jnp.concatenate([d[1], d[1]], axis=0) for d in dirs]
    eye = (lax.broadcasted_iota(jnp.int32, (LANES, LANES), 0)
           == lax.broadcasted_iota(jnp.int32, (LANES, LANES), 1))

    stf_ref[...] = s0_ref[0]
    stb_ref[...] = jnp.zeros_like(stb_ref)

    def step(t, carry):
        chains = []
        for d in range(2):
            for j in range(GLA_GROUP):
                idx = t * GLA_GROUP + j
                n = idx if d == 0 else n_chunks - 1 - idx
                chains.append({"d": d, "rows": pl.ds(pl.multiple_of(n * GLA_CHUNK, GLA_CHUNK), GLA_CHUNK)})
        for ch in chains:
            g = dirs[ch["d"]][0][0, ch["rows"], :]
            g_hi = g.astype(BF16)
            g_lo = (g - g_hi.astype(F32)).astype(BF16)
            ch["b2"] = _dot(tris[ch["d"]], jnp.concatenate([g_hi, g_lo], axis=1))
        for ch in chains:
            last = dirs[ch["d"]][2]
            b = ch["b2"][:, :LANES] + ch["b2"][:, LANES:]
            b_last = b[last:last + 1, :]
            q_e = gq_ref[0, ch["rows"], :] * jnp.exp(b)
            k = gk_ref[0, ch["rows"], :]
            ch["qm2"] = jnp.concatenate([q_e * masks[0], q_e * masks[1]], axis=0).astype(BF16)
            ch["k_e"] = (k * jnp.exp(-b)).astype(BF16)
            ch["k_d"] = (k * jnp.exp(b_last - b)).astype(BF16)
            ch["decay"] = jnp.sum(jnp.where(eye, jnp.exp(b_last), 0.0), axis=1, keepdims=True)
        for ch in chains:
            ch["a2"] = _dot_nt(ch["qm2"], ch["k_e"])
            ch["kv"] = _dot_tn(ch["k_d"], gv_ref[0, ch["rows"], :])
        for d in range(2):
            st_ref = dirs[d][3]
            st = st_ref[...]
            for ch in chains:
                if ch["d"] == d:
                    ch["st_prev"] = st.astype(BF16)
                    st = ch["decay"] * st + ch["kv"]
            st_ref[...] = st
        for ch in chains:
            d = ch["d"]
            a2 = jnp.where(keep2[d], ch["a2"], 0.0).astype(BF16)
            o2 = _dot(a2, gv_ref[0, ch["rows"], :]) + _dot(ch["qm2"], ch["st_prev"])
            for hh in range(2):
                rows_h = slice(GLA_CHUNK * hh, GLA_CHUNK * (hh + 1))
                cols = slice(GLA_DV * hh, GLA_DV * (hh + 1))
                dirs[d][4][ch["rows"], cols] = o2[rows_h, cols]
        return carry

    lax.fori_loop(0, n_chunks // GLA_GROUP, step, 0)

    def finish(i, carry):
        rows = pl.ds(pl.multiple_of(i * GLA_FINAL_ROWS, GLA_FINAL_ROWS), GLA_FINAL_ROWS)
        for hh in range(2):
            cols = slice(GLA_DV * hh, GLA_DV * (hh + 1))
            o = _rms(ofwd_ref[rows, cols] + obwd_ref[rows, cols], gain_ref[...])
            o_ref[0, rows, cols] = (o * gr_ref[0, rows, cols]).astype(BF16)
        return carry

    lax.fori_loop(0, seq // GLA_FINAL_ROWS, finish, 0)


def _gla(gq, gk, gv, gf, gb, gr, s0, gain):
    b, n, _ = gq.shape
    assert (n // GLA_CHUNK) % GLA_GROUP == 0 and n % GLA_FINAL_ROWS == 0
    n_pairs = GLA_HEADS // 2
    key_spec = pl.BlockSpec((1, n, LANES), lambda bi, pi: (bi, 0, pi))
    val_spec = pl.BlockSpec((1, n, 2 * GLA_DV), lambda bi, pi: (bi, 0, pi))
    return pl.pallas_call(
        _gla_kernel,
        grid=(b, n_pairs),
        in_specs=[key_spec, key_spec, val_spec, key_spec, key_spec, val_spec,
                  pl.BlockSpec((1, LANES, 2 * GLA_DV), lambda bi, pi: (pi, 0, 0)),
                  pl.BlockSpec((1, GLA_DV), lambda bi, pi: (0, 0))],
        out_specs=val_spec,
        out_shape=jax.ShapeDtypeStruct((b, n, GLA_VAL_WIDTH), BF16),
        scratch_shapes=[pltpu.VMEM((n, 2 * GLA_DV), F32), pltpu.VMEM((n, 2 * GLA_DV), F32),
                        pltpu.VMEM((LANES, 2 * GLA_DV), F32), pltpu.VMEM((LANES, 2 * GLA_DV), F32)],
        compiler_params=pltpu.CompilerParams(dimension_semantics=("arbitrary", "arbitrary"),
                                             vmem_limit_bytes=VMEM_LIMIT),
        name="gla",
    )(gq, gk, gv, gf, gb, gr, s0, gain)


def _out_ffn_kernel(x_ref, na_ref, gla_ref, wout_ref, gffn_ref, wg_ref, wu_ref, wd_ref, gfin_ref, o_ref):
    subs = [slice(r, r + SUB_ROWS) for r in range(0, x_ref.shape[0], SUB_ROWS)]
    mixes = [_dot(na_ref[rows, :], wout_ref[0:NA_WIDTH, :]) + _dot(gla_ref[rows, :], wout_ref[NA_WIDTH:, :])
             for rows in subs]
    hs = [x_ref[rows, :] + mix for rows, mix in zip(subs, mixes)]
    hns = [_rms(h, gffn_ref[...]).astype(BF16) for h in hs]
    acts = [(_silu(_dot(hn, wg_ref[...])) * _dot(hn, wu_ref[...])).astype(BF16) for hn in hns]
    outs = [h + _dot(act, wd_ref[...]) for h, act in zip(hs, acts)]
    for rows, out in zip(subs, outs):
        o_ref[rows, :] = _rms(out, gfin_ref[...])


def _out_ffn(x2d, na2d, gla2d, w_out, g_ffn, w_gate, w_up, w_down, g_fin, tm):
    m, d = x2d.shape
    const = lambda shape: pl.BlockSpec(shape, lambda i: (0, 0))
    row = lambda width: pl.BlockSpec((tm, width), lambda i: (i, 0))
    return pl.pallas_call(
        _out_ffn_kernel,
        grid=(m // tm,),
        in_specs=[row(d), row(NA_WIDTH), row(GLA_VAL_WIDTH), const(w_out.shape), const((1, d)),
                  const(w_gate.shape), const(w_up.shape), const(w_down.shape), const((1, d))],
        out_specs=row(d),
        out_shape=jax.ShapeDtypeStruct((m, d), F32),
        compiler_params=pltpu.CompilerParams(dimension_semantics=("arbitrary",),
                                             vmem_limit_bytes=56 * 1024 * 1024),
        name="out_ffn",
    )(x2d, na2d, gla2d, w_out, g_ffn, w_gate, w_up, w_down, g_fin)


def _na_bias_pairs(rpb):
    cols = np.arange(GRID_W)
    col_start = np.clip(cols - NA_KW // 2, 0, GRID_W - NA_KW)
    valid = (cols[None, :] >= col_start[:, None]) & (cols[None, :] < col_start[:, None] + NA_KW)
    dc = np.clip(cols[None, :] - cols[:, None], -(NA_KW - 1), NA_KW - 1) + (NA_KW - 1)
    onehot = (dc[None] == np.arange(2 * NA_KW - 1)[:, None, None]).astype(np.float32)
    expanded = jnp.einsum("hro,oqk->hrqk", rpb, onehot, precision=lax.Precision.HIGHEST)
    tiles = jnp.where(valid[None, None], expanded * LOG2_E, NEG_INF).astype(F32)
    two_rows = jnp.concatenate([tiles[:, :-1], tiles[:, 1:]], axis=-1)
    n_dr = two_rows.shape[1]
    stacked = two_rows.reshape(NA_HEADS // 2, 2, n_dr, GRID_W, LANES).transpose(0, 2, 1, 3, 4)
    return stacked.reshape(NA_HEADS // 2, n_dr, 2 * GRID_W, LANES)


def _na_meta_bias_pairs(meta_bias):
    return jnp.repeat(meta_bias.reshape(NA_HEADS // 2, 2, N_META) * LOG2_E, GRID_W, axis=1)


def kernel(x, meta_tokens, norm_mix_gain, w_in, rpb, meta_bias, w_gate_up_fwd, b_gate_fwd, w_gate_up_bwd, b_gate_bwd, gla_norm_gain, w_out, norm_ffn_gain, w_ffn_gate, w_ffn_up, w_ffn_down, norm_final_gain):
    bsz, seq, d = x.shape
    assert w_in.shape[0] == 1, "single-layer block: meta-token outputs are never consumed"
    assert w_in.shape[2] == IN_WIDTH and seq % GRID_W == 0 and GLA_CHUNK == GRID_W

    w_pad = jnp.concatenate([w_in[0], jnp.zeros((d, IN_WIDTH_PAD - IN_WIDTH), F32)], axis=1).astype(BF16)
    zeros_up = jnp.zeros((GATE_SLAB - 2 * GLA_GATE_RANK, GLA_KEY_WIDTH), F32)
    wupf = jnp.concatenate([w_gate_up_fwd[0], jnp.zeros_like(w_gate_up_bwd[0]), zeros_up], axis=0).astype(BF16)
    wupb = jnp.concatenate([jnp.zeros_like(w_gate_up_fwd[0]), w_gate_up_bwd[0], zeros_up], axis=0).astype(BF16)
    bgf = b_gate_fwd[0][None, :]
    bgb = b_gate_bwd[0][None, :]
    gain_mix = norm_mix_gain[0][None, :]

    x2d = x.reshape(bsz * seq, d)
    proj_args = (gain_mix, w_pad, wupf, bgf, wupb, bgb)
    naq, nak, nav, gq, gk, gv, gr, gf, gb = _in_proj(x2d, *proj_args, tm=1024)
    _, km, vm, _, gk_m, gv_m, _, gf_m, _ = _in_proj(meta_tokens, *proj_args, tm=N_META)

    s0 = _gla_init(gk_m, gv_m, gf_m)

    per_batch = lambda a: a.reshape(bsz, seq, a.shape[-1])
    na_out = _na(per_batch(naq), per_batch(nak), per_batch(nav), km, vm,
                 _na_bias_pairs(rpb[0]), _na_meta_bias_pairs(meta_bias[0]))
    gla_out = _gla(per_batch(gq), per_batch(gk), per_batch(gv), per_batch(gf), per_batch(gb),
                   per_batch(gr), s0, gla_norm_gain[0][None, :])

    out = _out_ffn(x2d, na_out.reshape(bsz * seq, NA_WIDTH), gla_out.reshape(bsz * seq, GLA_VAL_WIDTH),
                   w_out[0].astype(BF16), norm_ffn_gain[0][None, :], w_ffn_gate[0].astype(BF16),
                   w_ffn_up[0].astype(BF16), w_ffn_down[0].astype(BF16), norm_final_gain[None, :], tm=512)
    return out.reshape(bsz, seq, d)
```

```python
import functools

import jax
import jax.numpy as jnp
import numpy as np
from jax import lax
from jax.experimental import pallas as pl
from jax.experimental.pallas import tpu as pltpu

F32 = jnp.float32
BF16 = jnp.bfloat16

N_META = 16
GRID_W = 64
NA_HEADS = 8
NA_HEAD_DIM = 64
NA_WIDTH = NA_HEADS * NA_HEAD_DIM
NA_KH_MAX = 8
NA_KW = 16
GLA_HEADS = 4
GLA_DK = 64
GLA_DV = 128
GLA_KEY_WIDTH = GLA_HEADS * GLA_DK
GLA_VAL_WIDTH = GLA_HEADS * GLA_DV
GLA_GATE_RANK = 16
GLA_GATE_TAU = 16.0
GLA_CHUNK = 64
RMS_EPS = 1e-6
NEG_INF = -1e30
LOG2_E = 1.4426950408889634
NA_Q_SCALE = NA_HEAD_DIM ** -0.5 * LOG2_E

LANES = 128
GATE_SLAB = LANES
VMEM_LIMIT = 48 * 1024 * 1024
MXU_TILE = 256
FFN_CHUNK = 4 * MXU_TILE
NA_ROWS_PER_STEP = 4
SUB_ROWS = 256

_OFF_NA_Q = 0
_OFF_NA_K = _OFF_NA_Q + NA_WIDTH
_OFF_NA_V = _OFF_NA_K + NA_WIDTH
_OFF_G_Q = _OFF_NA_V + NA_WIDTH
_OFF_G_K = _OFF_G_Q + GLA_KEY_WIDTH
_OFF_G_V = _OFF_G_K + GLA_KEY_WIDTH
_OFF_G_R = _OFF_G_V + GLA_VAL_WIDTH
_OFF_GATE = _OFF_G_R + GLA_VAL_WIDTH
IN_WIDTH = _OFF_GATE + 2 * GLA_GATE_RANK
IN_WIDTH_PAD = _OFF_GATE + GATE_SLAB


def _rms(x, gain):
    ms = jnp.mean(x * x, axis=-1, keepdims=True)
    return x * lax.rsqrt(ms + RMS_EPS) * gain


def _silu(x):
    return x * (1.0 / (1.0 + jnp.exp(-x)))


def _log_sigmoid(x):
    return jnp.minimum(x, 0.0) - jnp.log1p(jnp.exp(-jnp.abs(x)))


def _dot(a, b):
    return jnp.dot(a, b, preferred_element_type=F32)


def _dot_nt(a, b):
    return lax.dot_general(a, b, (((1,), (1,)), ((), ())), preferred_element_type=F32)


def _dot_tn(a, b):
    return lax.dot_general(a, b, (((0,), (0,)), ((), ())), preferred_element_type=F32)


def _head_masks(rows, dtype):
    lane = lax.broadcasted_iota(jnp.int32, (rows, LANES), 1)
    lo = jnp.where(lane < LANES // 2, 1.0, 0.0).astype(dtype)
    hi = jnp.where(lane >= LANES // 2, 1.0, 0.0).astype(dtype)
    return lo, hi


def _in_proj_kernel(x_ref, gain_ref, w_ref, wupf_ref, bf_ref, wupb_ref, bb_ref,
                    naq_ref, nak_ref, nav_ref, gq_ref, gk_ref, gv_ref, gr_ref, gf_ref, gb_ref):
    tm = x_ref.shape[0]
    sub = min(tm, 2 * SUB_ROWS)
    subs = [slice(r, r + sub) for r in range(0, tm, sub)]
    xns = [_rms(x_ref[rows, :], gain_ref[...]).astype(BF16) for rows in subs]
    for rows, xn in zip(subs, xns):

        def proj(lo, width):
            return _dot(xn, w_ref[:, lo:lo + width])

        half = GLA_VAL_WIDTH // 2
        low = proj(_OFF_GATE, GATE_SLAB).astype(BF16)
        naq_ref[rows, :] = (proj(_OFF_NA_Q, NA_WIDTH) * NA_Q_SCALE).astype(BF16)
        gr_ref[rows, 0:half] = _silu(proj(_OFF_G_R, half))
        gq_ref[rows, :] = proj(_OFF_G_Q, GLA_KEY_WIDTH) * (GLA_DK ** -0.5)
        gf_ref[rows, :] = _log_sigmoid(_dot(low, wupf_ref[...]) + bf_ref[...]) * (1.0 / GLA_GATE_TAU)
        gv_ref[rows, :] = proj(_OFF_G_V, GLA_VAL_WIDTH).astype(BF16)
        gr_ref[rows, half:] = _silu(proj(_OFF_G_R + half, half))
        gk_ref[rows, :] = proj(_OFF_G_K, GLA_KEY_WIDTH)
        gb_ref[rows, :] = _log_sigmoid(_dot(low, wupb_ref[...]) + bb_ref[...]) * (1.0 / GLA_GATE_TAU)
        nak_ref[rows, :] = proj(_OFF_NA_K, NA_WIDTH).astype(BF16)
        nav_ref[rows, :] = proj(_OFF_NA_V, NA_WIDTH).astype(BF16)


def _in_proj(x2d, gain, w_pad, wupf, bgf, wupb, bgb, tm):
    m, d = x2d.shape
    const = lambda shape: pl.BlockSpec(shape, lambda i: (0, 0))
    row = lambda width: pl.BlockSpec((tm, width), lambda i: (i, 0))
    widths = (NA_WIDTH, NA_WIDTH, NA_WIDTH, GLA_KEY_WIDTH, GLA_KEY_WIDTH,
              GLA_VAL_WIDTH, GLA_VAL_WIDTH, GLA_KEY_WIDTH, GLA_KEY_WIDTH)
    dtypes = (BF16, BF16, BF16, F32, F32, BF16, F32, F32, F32)
    return pl.pallas_call(
        _in_proj_kernel,
        grid=(m // tm,),
        in_specs=[row(d), const((1, d)), const(w_pad.shape), const(wupf.shape), const(bgf.shape),
                  const(wupb.shape), const(bgb.shape)],
        out_specs=[row(w) for w in widths],
        out_shape=[jax.ShapeDtypeStruct((m, w), dt) for w, dt in zip(widths, dtypes)],
        compiler_params=pltpu.CompilerParams(dimension_semantics=("arbitrary",),
                                             vmem_limit_bytes=VMEM_LIMIT),
        name="in_proj",
    )(x2d, gain, w_pad, wupf, bgf, wupb, bgb)


def _gla_init_kernel(gk_ref, gv_ref, gf_ref, st_ref):
    n = gk_ref.shape[0]
    r = lax.broadcasted_iota(jnp.int32, (n, n), 0)
    c = lax.broadcasted_iota(jnp.int32, (n, n), 1)
    later = jnp.where(c > r, 1.0, 0.0).astype(BF16)
    g = gf_ref[...]
    g_hi = g.astype(BF16)
    g_lo = (g - g_hi.astype(F32)).astype(BF16)
    tail = _dot(later, g_hi) + _dot(later, g_lo)
    k_d = (gk_ref[...] * jnp.exp(tail)).astype(BF16)
    for p in range(GLA_HEADS // 2):
        st_ref[p] = _dot_tn(k_d[:, LANES * p:LANES * (p + 1)], gv_ref[:, 2 * GLA_DV * p:2 * GLA_DV * (p + 1)])


def _gla_init(gk_m, gv_m, gf_m):
    return pl.pallas_call(
        _gla_init_kernel,
        out_shape=jax.ShapeDtypeStruct((GLA_HEADS // 2, LANES, 2 * GLA_DV), F32),
        name="gla_init",
    )(gk_m, gv_m, gf_m)


def _na_kernel(q_ref, k_ref, v_ref, km_ref, vm_ref, bias_ref, mb_ref, o_ref, sw_ref, sm_ref, *, n_rows):
    kh = min(NA_KH_MAX, n_rows)
    n_pairs = NA_HEADS // 2
    pair_lanes = [slice(LANES * p, LANES * (p + 1)) for p in range(n_pairs)]
    masks = _head_masks(GRID_W, BF16)
    sel_lo = lax.broadcasted_iota(jnp.int32, (GRID_W, LANES), 1) < LANES // 2

    def window_start(i):
        return jnp.clip(i - kh // 2, 0, n_rows - kh)

    def row_scores(i, slot):
        qoff = pl.multiple_of(i * GRID_W, GRID_W)
        koff = pl.multiple_of(window_start(i) * GRID_W, GRID_W)
        for p in range(n_pairs):
            q_pair = q_ref[0, pl.ds(qoff, GRID_W), pair_lanes[p]]
            q2 = jnp.concatenate([q_pair * masks[0], q_pair * masks[1]], axis=0)
            k_win = k_ref[0, pl.ds(koff, kh * GRID_W), pair_lanes[p]]
            sw_ref[slot, p] = _dot_nt(q2, k_win)
            sm_ref[slot, p] = _dot_nt(q2, km_ref[:, pair_lanes[p]])

    def row_finish(i, slot):
        s = window_start(i)
        dr0 = s - i + (NA_KH_MAX - 1)
        qoff = pl.multiple_of(i * GRID_W, GRID_W)
        koff = pl.multiple_of(s * GRID_W, GRID_W)
        biased = []
        for p in range(n_pairs):
            s_meta = sm_ref[slot, p] + mb_ref[p]
            chunks = [sw_ref[slot, p, :, LANES * t:LANES * (t + 1)] + bias_ref[p, dr0 + 2 * t]
                      for t in range(kh * GRID_W // LANES)]
            m_el = functools.reduce(jnp.maximum, chunks)
            m = jnp.maximum(jnp.max(m_el, axis=-1, keepdims=True),
                            jnp.max(s_meta, axis=-1, keepdims=True))
            biased.append((chunks, s_meta, m))
        probs = []
        for p in range(n_pairs):
            chunks, s_meta, m = biased[p]
            p_chunks = [jnp.exp2(c - m) for c in chunks]
            p_meta = jnp.exp2(s_meta - m)
            l = (jnp.sum(functools.reduce(jnp.add, p_chunks), axis=-1, keepdims=True)
                 + jnp.sum(p_meta, axis=-1, keepdims=True))
            probs.append((jnp.concatenate(p_chunks, axis=1).astype(BF16), p_meta.astype(BF16), l))
        pair_outs = []
        for p in range(n_pairs):
            p_win, p_meta, l = probs[p]
            v_win = v_ref[0, pl.ds(koff, kh * GRID_W), pair_lanes[p]]
            o2 = (_dot(p_win, v_win) + _dot(p_meta, vm_ref[:, pair_lanes[p]])) / l
            pair_outs.append(jnp.where(sel_lo, o2[:GRID_W], o2[GRID_W:]).astype(BF16))
        o_ref[0, pl.ds(qoff, GRID_W), :] = jnp.concatenate(pair_outs, axis=1)

    def rows_step(j, carry):
        a = NA_ROWS_PER_STEP * j
        for u in range(NA_ROWS_PER_STEP):
            row_scores(jnp.minimum(a + u + 1, n_rows - 1), (u + 1) % 2)
            row_finish(a + u, u % 2)
        return carry

    row_scores(0, 0)
    lax.fori_loop(0, n_rows // NA_ROWS_PER_STEP, rows_step, 0)


def _na(q, k, v, k_meta, v_meta, bias_pairs, meta_bias):
    b, n, w = q.shape
    n_rows = n // GRID_W
    assert n_rows % NA_ROWS_PER_STEP == 0
    kh = min(NA_KH_MAX, n_rows)
    kern = functools.partial(_na_kernel, n_rows=n_rows)
    batch_spec = pl.BlockSpec((1, n, w), lambda bi: (bi, 0, 0))
    whole = lambda a: pl.BlockSpec(a.shape, lambda bi: (0,) * a.ndim)
    return pl.pallas_call(
        kern,
        grid=(b,),
        in_specs=[batch_spec, batch_spec, batch_spec, whole(k_meta), whole(v_meta), whole(bias_pairs),
                  whole(meta_bias)],
        out_specs=batch_spec,
        out_shape=jax.ShapeDtypeStruct((b, n, w), BF16),
        scratch_shapes=[pltpu.VMEM((2, NA_HEADS // 2, 2 * GRID_W, kh * GRID_W), F32),
                        pltpu.VMEM((2, NA_HEADS // 2, 2 * GRID_W, N_META), F32)],
        compiler_params=pltpu.CompilerParams(dimension_semantics=("arbitrary",),
                                             vmem_limit_bytes=VMEM_LIMIT),
        name="na",
    )(q, k, v, k_meta, v_meta, bias_pairs, meta_bias)


GLA_GROUP = 8
GLA_FINAL_ROWS = 1024


def _gla_kernel(gq_ref, gk_ref, gv_ref, gf_ref, gb_ref, gr_ref, s0_ref, gain_ref, o_ref,
                ofwd_ref, obwd_ref, stf_ref, stb_ref):
    seq = gq_ref.shape[1]
    n_chunks = seq // GLA_CHUNK
    r = lax.broadcasted_iota(jnp.int32, (GLA_CHUNK, GLA_CHUNK), 0)
    c = lax.broadcasted_iota(jnp.int32, (GLA_CHUNK, GLA_CHUNK), 1)
    masks = _head_masks(GLA_CHUNK, F32)
    dirs = ((gf_ref, c <= r, GLA_CHUNK - 1, stf_ref, ofwd_ref),
            (gb_ref, c >= r, 0, stb_ref, obwd_ref))
    tris = [jnp.where(d[1], 1.0, 0.0).astype(BF16) for d in dirs]
    keep2 =[jnp.concatenate([d[1], d[1]], axis=0) for d in dirs]
    eye = (lax.broadcasted_iota(jnp.int32, (LANES, LANES), 0)
           == lax.broadcasted_iota(jnp.int32, (LANES, LANES), 1))

    stf_ref[...] = s0_ref[0]
    stb_ref[...] = jnp.zeros_like(stb_ref)

    def step(t, carry):
        chains = []
        for d in range(2):
            for j in range(GLA_GROUP):
                idx = t * GLA_GROUP + j
                n = idx if d == 0 else n_chunks - 1 - idx
                chains.append({"d": d, "rows": pl.ds(pl.multiple_of(n * GLA_CHUNK, GLA_CHUNK), GLA_CHUNK)})
        for ch in chains:
            g = dirs[ch["d"]][0][0, ch["rows"], :]
            g_hi = g.astype(BF16)
            g_lo = (g - g_hi.astype(F32)).astype(BF16)
            ch["b2"] = _dot(tris[ch["d"]], jnp.concatenate([g_hi, g_lo], axis=1))
        for ch in chains:
            last = dirs[ch["d"]][2]
            b = ch["b2"][:, :LANES] + ch["b2"][:, LANES:]
            b_last = b[last:last + 1, :]
            q_e = gq_ref[0, ch["rows"], :] * jnp.exp(b)
            k = gk_ref[0, ch["rows"], :]
            ch["qm2"] = jnp.concatenate([q_e * masks[0], q_e * masks[1]], axis=0).astype(BF16)
            ch["k_e"] = (k * jnp.exp(-b)).astype(BF16)
            ch["k_d"] = (k * jnp.exp(b_last - b)).astype(BF16)
            ch["decay"] = jnp.sum(jnp.where(eye, jnp.exp(b_last), 0.0), axis=1, keepdims=True)
        for ch in chains:
            ch["a2"] = _dot_nt(ch["qm2"], ch["k_e"])
            ch["kv"] = _dot_tn(ch["k_d"], gv_ref[0, ch["rows"], :])
        for d in range(2):
            st_ref = dirs[d][3]
            st = st_ref[...]
            for ch in chains:
                if ch["d"] == d:
                    ch["st_prev"] = st.astype(BF16)
                    st = ch["decay"] * st + ch["kv"]
            st_ref[...] = st
        for ch in chains:
            d = ch["d"]
            a2 = jnp.where(keep2[d], ch["a2"], 0.0).astype(BF16)
            o2 = _dot(jnp.concatenate([ch["qm2"], a2], axis=1),
                      jnp.concatenate([ch["st_prev"], gv_ref[0, ch["rows"], :]], axis=0))
            for hh in range(2):
                rows_h = slice(GLA_CHUNK * hh, GLA_CHUNK * (hh + 1))
                cols = slice(GLA_DV * hh, GLA_DV * (hh + 1))
                dirs[d][4][ch["rows"], cols] = o2[rows_h, cols]
        return carry

    lax.fori_loop(0, n_chunks // GLA_GROUP, step, 0)

    def finish(i, carry):
        rows = pl.ds(pl.multiple_of(i * GLA_FINAL_ROWS, GLA_FINAL_ROWS), GLA_FINAL_ROWS)
        for hh in range(2):
            cols = slice(GLA_DV * hh, GLA_DV * (hh + 1))
            o = _rms(ofwd_ref[rows, cols] + obwd_ref[rows, cols], gain_ref[...])
            o_ref[0, rows, cols] = (o * gr_ref[0, rows, cols]).astype(BF16)
        return carry

    lax.fori_loop(0, seq // GLA_FINAL_ROWS, finish, 0)


def _gla(gq, gk, gv, gf, gb, gr, s0, gain):
    b, n, _ = gq.shape
    assert (n // GLA_CHUNK) % GLA_GROUP == 0 and n % GLA_FINAL_ROWS == 0
    n_pairs = GLA_HEADS // 2
    key_spec = pl.BlockSpec((1, n, LANES), lambda bi, pi: (bi, 0, pi))
    val_spec = pl.BlockSpec((1, n, 2 * GLA_DV), lambda bi, pi: (bi, 0, pi))
    return pl.pallas_call(
        _gla_kernel,
        grid=(b, n_pairs),
        in_specs=[key_spec, key_spec, val_spec, key_spec, key_spec, val_spec,
                  pl.BlockSpec((1, LANES, 2 * GLA_DV), lambda bi, pi: (pi, 0, 0)),
                  pl.BlockSpec((1, GLA_DV), lambda bi, pi: (0, 0))],
        out_specs=val_spec,
        out_shape=jax.ShapeDtypeStruct((b, n, GLA_VAL_WIDTH), BF16),
        scratch_shapes=[pltpu.VMEM((n, 2 * GLA_DV), F32), pltpu.VMEM((n, 2 * GLA_DV), F32),
                        pltpu.VMEM((LANES, 2 * GLA_DV), F32), pltpu.VMEM((LANES, 2 * GLA_DV), F32)],
        compiler_params=pltpu.CompilerParams(dimension_semantics=("arbitrary", "arbitrary"),
                                             vmem_limit_bytes=VMEM_LIMIT),
        name="gla",
    )(gq, gk, gv, gf, gb, gr, s0, gain)


def _out_ffn_kernel(x_ref, na_ref, gla_ref, wout_ref, gffn_ref, wg_ref, wu_ref, wd_ref, gfin_ref, o_ref):
    subs = [slice(r, r + SUB_ROWS) for r in range(0, x_ref.shape[0], SUB_ROWS)]
    mixes = [_dot(na_ref[rows, :], wout_ref[0:NA_WIDTH, :]) + _dot(gla_ref[rows, :], wout_ref[NA_WIDTH:, :])
             for rows in subs]
    hs = [x_ref[rows, :] + mix for rows, mix in zip(subs, mixes)]
    hns = [_rms(h, gffn_ref[...]).astype(BF16) for h in hs]
    acts = [(_silu(_dot(hn, wg_ref[...])) * _dot(hn, wu_ref[...])).astype(BF16) for hn in hns]
    outs = [h + _dot(act, wd_ref[...]) for h, act in zip(hs, acts)]
    for rows, out in zip(subs, outs):
        o_ref[rows, :] = _rms(out, gfin_ref[...])


def _out_ffn(x2d, na2d, gla2d, w_out, g_ffn, w_gate, w_up, w_down, g_fin, tm):
    m, d = x2d.shape
    const = lambda shape: pl.BlockSpec(shape, lambda i: (0, 0))
    row = lambda width: pl.BlockSpec((tm, width), lambda i: (i, 0))
    return pl.pallas_call(
        _out_ffn_kernel,
        grid=(m // tm,),
        in_specs=[row(d), row(NA_WIDTH), row(GLA_VAL_WIDTH), const(w_out.shape), const((1, d)),
                  const(w_gate.shape), const(w_up.shape), const(w_down.shape), const((1, d))],
        out_specs=row(d),
        out_shape=jax.ShapeDtypeStruct((m, d), F32),
        compiler_params=pltpu.CompilerParams(dimension_semantics=("arbitrary",),
                                             vmem_limit_bytes=56 * 1024 * 1024),
        name="out_ffn",
    )(x2d, na2d, gla2d, w_out, g_ffn, w_gate, w_up, w_down, g_fin)


def _na_bias_pairs(rpb):
    cols = np.arange(GRID_W)
    col_start = np.clip(cols - NA_KW // 2, 0, GRID_W - NA_KW)
    valid = (cols[None, :] >= col_start[:, None]) & (cols[None, :] < col_start[:, None] + NA_KW)
    dc = np.clip(cols[None, :] - cols[:, None], -(NA_KW - 1), NA_KW - 1) + (NA_KW - 1)
    onehot = (dc[None] == np.arange(2 * NA_KW - 1)[:, None, None]).astype(np.float32)
    expanded = jnp.einsum("hro,oqk->hrqk", rpb, onehot, precision=lax.Precision.HIGHEST)
    tiles = jnp.where(valid[None, None], expanded * LOG2_E, NEG_INF).astype(F32)
    two_rows = jnp.concatenate([tiles[:, :-1], tiles[:, 1:]], axis=-1)
    n_dr = two_rows.shape[1]
    stacked = two_rows.reshape(NA_HEADS // 2, 2, n_dr, GRID_W, LANES).transpose(0, 2, 1, 3, 4)
    return stacked.reshape(NA_HEADS // 2, n_dr, 2 * GRID_W, LANES)


def _na_meta_bias_pairs(meta_bias):
    return jnp.repeat(meta_bias.reshape(NA_HEADS // 2, 2, N_META) * LOG2_E, GRID_W, axis=1)


def kernel(x, meta_tokens, norm_mix_gain, w_in, rpb, meta_bias, w_gate_up_fwd, b_gate_fwd, w_gate_up_bwd, b_gate_bwd, gla_norm_gain, w_out, norm_ffn_gain, w_ffn_gate, w_ffn_up, w_ffn_down, norm_final_gain):
    bsz, seq, d = x.shape
    assert w_in.shape[0] == 1, "single-layer block: meta-token outputs are never consumed"
    assert w_in.shape[2] == IN_WIDTH and seq % GRID_W == 0 and GLA_CHUNK == GRID_W

    w_pad = jnp.concatenate([w_in[0], jnp.zeros((d, IN_WIDTH_PAD - IN_WIDTH), F32)], axis=1).astype(BF16)
    zeros_up = jnp.zeros((GATE_SLAB - 2 * GLA_GATE_RANK, GLA_KEY_WIDTH), F32)
    wupf = jnp.concatenate([w_gate_up_fwd[0], jnp.zeros_like(w_gate_up_bwd[0]), zeros_up], axis=0).astype(BF16)
    wupb = jnp.concatenate([jnp.zeros_like(w_gate_up_fwd[0]), w_gate_up_bwd[0], zeros_up], axis=0).astype(BF16)
    bgf = b_gate_fwd[0][None, :]
    bgb = b_gate_bwd[0][None, :]
    gain_mix = norm_mix_gain[0][None, :]

    x2d = x.reshape(bsz * seq, d)
    proj_args = (gain_mix, w_pad, wupf, bgf, wupb, bgb)
    naq, nak, nav, gq, gk, gv, gr, gf, gb = _in_proj(x2d, *proj_args, tm=1024)
    _, km, vm, _, gk_m, gv_m, _, gf_m, _ = _in_proj(meta_tokens, *proj_args, tm=N_META)

    s0 = _gla_init(gk_m, gv_m, gf_m)

    per_batch = lambda a: a.reshape(bsz, seq, a.shape[-1])
    na_out = _na(per_batch(naq), per_batch(nak), per_batch(nav), km, vm,
                 _na_bias_pairs(rpb[0]), _na_meta_bias_pairs(meta_bias[0]))
    gla_out = _gla(per_batch(gq), per_batch(gk), per_batch(gv), per_batch(gf), per_batch(gb),
                   per_batch(gr), s0, gla_norm_gain[0][None, :])

    out = _out_ffn(x2d, na_out.reshape(bsz * seq, NA_WIDTH), gla_out.reshape(bsz * seq, GLA_VAL_WIDTH),
                   w_out[0].astype(BF16), norm_ffn_gain[0][None, :], w_ffn_gate[0].astype(BF16),
                   w_ffn_up[0].astype(BF16), w_ffn_down[0].astype(BF16), norm_final_gain[None, :], tm=512)
    return out.reshape(bsz, seq, d)
```

```python
import functools

import jax
import jax.numpy as jnp
import numpy as np
from jax import lax
from jax.experimental import pallas as pl
from jax.experimental.pallas import tpu as pltpu

F32 = jnp.float32
BF16 = jnp.bfloat16

N_META = 16
GRID_W = 64
NA_HEADS = 8
NA_HEAD_DIM = 64
NA_WIDTH = NA_HEADS * NA_HEAD_DIM
NA_KH_MAX = 8
NA_KW = 16
GLA_HEADS = 4
GLA_DK = 64
GLA_DV = 128
GLA_KEY_WIDTH = GLA_HEADS * GLA_DK
GLA_VAL_WIDTH = GLA_HEADS * GLA_DV
GLA_GATE_RANK = 16
GLA_GATE_TAU = 16.0
GLA_CHUNK = 64
RMS_EPS = 1e-6
NEG_INF = -1e30
LOG2_E = 1.4426950408889634
NA_Q_SCALE = NA_HEAD_DIM ** -0.5 * LOG2_E

LANES = 128
GATE_SLAB = LANES
VMEM_LIMIT = 48 * 1024 * 1024
MXU_TILE = 256
FFN_CHUNK = 4 * MXU_TILE
NA_ROWS_PER_STEP = 4
SUB_ROWS = 256

_OFF_NA_Q = 0
_OFF_NA_K = _OFF_NA_Q + NA_WIDTH
_OFF_NA_V = _OFF_NA_K + NA_WIDTH
_OFF_G_Q = _OFF_NA_V + NA_WIDTH
_OFF_G_K = _OFF_G_Q + GLA_KEY_WIDTH
_OFF_G_V = _OFF_G_K + GLA_KEY_WIDTH
_OFF_G_R = _OFF_G_V + GLA_VAL_WIDTH
_OFF_GATE = _OFF_G_R + GLA_VAL_WIDTH
IN_WIDTH = _OFF_GATE + 2 * GLA_GATE_RANK
IN_WIDTH_PAD = _OFF_GATE + GATE_SLAB


def _rms(x, gain):
    ms = jnp.mean(x * x, axis=-1, keepdims=True)
    return x * lax.rsqrt(ms + RMS_EPS) * gain


def _silu(x):
    return x * (1.0 / (1.0 + jnp.exp(-x)))


def _log_sigmoid(x):
    return jnp.minimum(x, 0.0) - jnp.log(1.0 + jnp.exp(-jnp.abs(x)))


def _dot(a, b):
    return jnp.dot(a, b, preferred_element_type=F32)


def _dot_nt(a, b):
    return lax.dot_general(a, b, (((1,), (1,)), ((), ())), preferred_element_type=F32)


def _dot_tn(a, b):
    return lax.dot_general(a, b, (((0,), (0,)), ((), ())), preferred_element_type=F32)


def _head_masks(rows, dtype):
    lane = lax.broadcasted_iota(jnp.int32, (rows, LANES), 1)
    lo = jnp.where(lane < LANES // 2, 1.0, 0.0).astype(dtype)
    hi = jnp.where(lane >= LANES // 2, 1.0, 0.0).astype(dtype)
    return lo, hi


def _in_proj_kernel(x_ref, gain_ref, w_ref, wupf_ref, bf_ref, wupb_ref, bb_ref,
                    naq_ref, nak_ref, nav_ref, gq_ref, gk_ref, gv_ref, gr_ref, gf_ref, gb_ref):
    tm = x_ref.shape[0]
    sub = min(tm, 2 * SUB_ROWS)
    subs = [slice(r, r + sub) for r in range(0, tm, sub)]
    xns = [_rms(x_ref[rows, :], gain_ref[...]).astype(BF16) for rows in subs]
    for rows, xn in zip(subs, xns):

        def proj(lo, width):
            return _dot(xn, w_ref[:, lo:lo + width])

        half = GLA_VAL_WIDTH // 2
        low = proj(_OFF_GATE, GATE_SLAB).astype(BF16)
        naq_ref[rows, :] = (proj(_OFF_NA_Q, NA_WIDTH) * NA_Q_SCALE).astype(BF16)
        gr_ref[rows, 0:half] = _silu(proj(_OFF_G_R, half))
        gq_ref[rows, :] = proj(_OFF_G_Q, GLA_KEY_WIDTH) * (GLA_DK ** -0.5)
        gf_ref[rows, :] = _log_sigmoid(_dot(low, wupf_ref[...]) + bf_ref[...]) * (1.0 / GLA_GATE_TAU)
        gv_ref[rows, :] = proj(_OFF_G_V, GLA_VAL_WIDTH).astype(BF16)
        gr_ref[rows, half:] = _silu(proj(_OFF_G_R + half, half))
        gk_ref[rows, :] = proj(_OFF_G_K, GLA_KEY_WIDTH)
        gb_ref[rows, :] = _log_sigmoid(_dot(low, wupb_ref[...]) + bb_ref[...]) * (1.0 / GLA_GATE_TAU)
        nak_ref[rows, :] = proj(_OFF_NA_K, NA_WIDTH).astype(BF16)
        nav_ref[rows, :] = proj(_OFF_NA_V, NA_WIDTH).astype(BF16)


def _in_proj(x2d, gain, w_pad, wupf, bgf, wupb, bgb, tm):
    m, d = x2d.shape
    const = lambda shape: pl.BlockSpec(shape, lambda i: (0, 0))
    row = lambda width: pl.BlockSpec((tm, width), lambda i: (i, 0))
    widths = (NA_WIDTH, NA_WIDTH, NA_WIDTH, GLA_KEY_WIDTH, GLA_KEY_WIDTH,
              GLA_VAL_WIDTH, GLA_VAL_WIDTH, GLA_KEY_WIDTH, GLA_KEY_WIDTH)
    dtypes = (BF16, BF16, BF16, F32, F32, BF16, F32, F32, F32)
    return pl.pallas_call(
        _in_proj_kernel,
        grid=(m // tm,),
        in_specs=[row(d), const((1, d)), const(w_pad.shape), const(wupf.shape), const(bgf.shape),
                  const(wupb.shape), const(bgb.shape)],
        out_specs=[row(w) for w in widths],
        out_shape=[jax.ShapeDtypeStruct((m, w), dt) for w, dt in zip(widths, dtypes)],
        compiler_params=pltpu.CompilerParams(dimension_semantics=("arbitrary",),
                                             vmem_limit_bytes=VMEM_LIMIT),
        name="in_proj",
    )(x2d, gain, w_pad, wupf, bgf, wupb, bgb)


def _gla_init_kernel(gk_ref, gv_ref, gf_ref, st_ref):
    n = gk_ref.shape[0]
    r = lax.broadcasted_iota(jnp.int32, (n, n), 0)
    c = lax.broadcasted_iota(jnp.int32, (n, n), 1)
    later = jnp.where(c > r, 1.0, 0.0).astype(BF16)
    g = gf_ref[...]
    g_hi = g.astype(BF16)
    g_lo = (g - g_hi.astype(F32)).astype(BF16)
    tail = _dot(later, g_hi) + _dot(later, g_lo)
    k_d = (gk_ref[...] * jnp.exp(tail)).astype(BF16)
    for p in range(GLA_HEADS // 2):
        st_ref[p] = _dot_tn(k_d[:, LANES * p:LANES * (p + 1)], gv_ref[:, 2 * GLA_DV * p:2 * GLA_DV * (p + 1)])


def _gla_init(gk_m, gv_m, gf_m):
    return pl.pallas_call(
        _gla_init_kernel,
        out_shape=jax.ShapeDtypeStruct((GLA_HEADS // 2, LANES, 2 * GLA_DV), F32),
        name="gla_init",
    )(gk_m, gv_m, gf_m)


def _na_kernel(q_ref, k_ref, v_ref, km_ref, vm_ref, bias_ref, mb_ref, o_ref, sw_ref, sm_ref, *, n_rows):
    kh = min(NA_KH_MAX, n_rows)
    n_pairs = NA_HEADS // 2
    pair_lanes = [slice(LANES * p, LANES * (p + 1)) for p in range(n_pairs)]
    masks = _head_masks(GRID_W, BF16)
    sel_lo = lax.broadcasted_iota(jnp.int32, (GRID_W, LANES), 1) < LANES // 2

    def window_start(i):
        return jnp.clip(i - kh // 2, 0, n_rows - kh)

    def row_scores(i, slot):
        qoff = pl.multiple_of(i * GRID_W, GRID_W)
        koff = pl.multiple_of(window_start(i) * GRID_W, GRID_W)
        for p in range(n_pairs):
            q_pair = q_ref[0, pl.ds(qoff, GRID_W), pair_lanes[p]]
            q2 = jnp.concatenate([q_pair * masks[0], q_pair * masks[1]], axis=0)
            k_win = k_ref[0, pl.ds(koff, kh * GRID_W), pair_lanes[p]]
            sw_ref[slot, p] = _dot_nt(q2, k_win)
            sm_ref[slot, p] = _dot_nt(q2, km_ref[:, pair_lanes[p]])

    def row_finish(i, slot):
        s = window_start(i)
        dr0 = s - i + (NA_KH_MAX - 1)
        qoff = pl.multiple_of(i * GRID_W, GRID_W)
        koff = pl.multiple_of(s * GRID_W, GRID_W)
        biased = []
        for p in range(n_pairs):
            s_meta = sm_ref[slot, p] + mb_ref[p]
            chunks = [sw_ref[slot, p, :, LANES * t:LANES * (t + 1)] + bias_ref[p, dr0 + 2 * t]
                      for t in range(kh * GRID_W // LANES)]
            m_el = functools.reduce(jnp.maximum, chunks)
            m = jnp.maximum(jnp.max(m_el, axis=-1, keepdims=True),
                            jnp.max(s_meta, axis=-1, keepdims=True))
            biased.append((chunks, s_meta, m))
        probs = []
        for p in range(n_pairs):
            chunks, s_meta, m = biased[p]
            p_chunks = [jnp.exp2(c - m) for c in chunks]
            p_meta = jnp.exp2(s_meta - m)
            l = (jnp.sum(functools.reduce(jnp.add, p_chunks), axis=-1, keepdims=True)
                 + jnp.sum(p_meta, axis=-1, keepdims=True))
            probs.append((jnp.concatenate(p_chunks, axis=1).astype(BF16), p_meta.astype(BF16), l))
        pair_outs = []
        for p in range(n_pairs):
            p_win, p_meta, l = probs[p]
            v_win = v_ref[0, pl.ds(koff, kh * GRID_W), pair_lanes[p]]
            o2 = (_dot(p_win, v_win) + _dot(p_meta, vm_ref[:, pair_lanes[p]])) / l
            pair_outs.append(jnp.where(sel_lo, o2[:GRID_W], o2[GRID_W:]).astype(BF16))
        o_ref[0, pl.ds(qoff, GRID_W), :] = jnp.concatenate(pair_outs, axis=1)

    def rows_step(j, carry):
        a = NA_ROWS_PER_STEP * j
        for u in range(NA_ROWS_PER_STEP):
            row_scores(jnp.minimum(a + u + 1, n_rows - 1), (u + 1) % 2)
            row_finish(a + u, u % 2)
        return carry

    row_scores(0, 0)
    lax.fori_loop(0, n_rows // NA_ROWS_PER_STEP, rows_step, 0)


def _na(q, k, v, k_meta, v_meta, bias_pairs, meta_bias):
    b, n, w = q.shape
    n_rows = n // GRID_W
    assert n_rows % NA_ROWS_PER_STEP == 0
    kh = min(NA_KH_MAX, n_rows)
    kern = functools.partial(_na_kernel, n_rows=n_rows)
    batch_spec = pl.BlockSpec((1, n, w), lambda bi: (bi, 0, 0))
    whole = lambda a: pl.BlockSpec(a.shape, lambda bi: (0,) * a.ndim)
    return pl.pallas_call(
        kern,
        grid=(b,),
        in_specs=[batch_spec, batch_spec, batch_spec, whole(k_meta), whole(v_meta), whole(bias_pairs),
                  whole(meta_bias)],
        out_specs=batch_spec,
        out_shape=jax.ShapeDtypeStruct((b, n, w), BF16),
        scratch_shapes=[pltpu.VMEM((2, NA_HEADS // 2, 2 * GRID_W, kh * GRID_W), F32),
                        pltpu.VMEM((2, NA_HEADS // 2, 2 * GRID_W, N_META), F32)],
        compiler_params=pltpu.CompilerParams(dimension_semantics=("arbitrary",),
                                             vmem_limit_bytes=VMEM_LIMIT),
        name="na",
    )(q, k, v, k_meta, v_meta, bias_pairs, meta_bias)


GLA_GROUP = 8
GLA_FINAL_ROWS = 1024


def _gla_kernel(gq_ref, gk_ref, gv_ref, gf_ref, gb_ref, gr_ref, s0_ref, gain_ref, o_ref,
                ofwd_ref, obwd_ref, stf_ref, stb_ref):
    seq = gq_ref.shape[1]
    n_chunks = seq // GLA_CHUNK
    r = lax.broadcasted_iota(jnp.int32, (GLA_CHUNK, GLA_CHUNK), 0)
    c = lax.broadcasted_iota(jnp.int32, (GLA_CHUNK, GLA_CHUNK), 1)
    masks = _head_masks(GLA_CHUNK, F32)
    dirs = ((gf_ref, c <= r, GLA_CHUNK - 1, stf_ref, ofwd_ref),
            (gb_ref, c >= r, 0, stb_ref, obwd_ref))
    tris = [jnp.where(d[1], 1.0, 0.0).astype(BF16) for d in dirs]
    keep2 =[jnp.concatenate([d[1], d[1]], axis=0) for d in dirs]
    eye = (lax.broadcasted_iota(jnp.int32, (LANES, LANES), 0)
           == lax.broadcasted_iota(jnp.int32, (LANES, LANES), 1))

    stf_ref[...] = s0_ref[0]
    stb_ref[...] = jnp.zeros_like(stb_ref)

    def step(t, carry):
        chains = []
        for d in range(2):
            for j in range(GLA_GROUP):
                idx = t * GLA_GROUP + j
                n = idx if d == 0 else n_chunks - 1 - idx
                chains.append({"d": d, "rows": pl.ds(pl.multiple_of(n * GLA_CHUNK, GLA_CHUNK), GLA_CHUNK)})
        for ch in chains:
            g = dirs[ch["d"]][0][0, ch["rows"], :]
            g_hi = g.astype(BF16)
            g_lo = (g - g_hi.astype(F32)).astype(BF16)
            ch["b2"] = _dot(tris[ch["d"]], jnp.concatenate([g_hi, g_lo], axis=1))
        for ch in chains:
            last = dirs[ch["d"]][2]
            b = ch["b2"][:, :LANES] + ch["b2"][:, LANES:]
            b_last = b[last:last + 1, :]
            q_e = gq_ref[0, ch["rows"], :] * jnp.exp(b)
            k = gk_ref[0, ch["rows"], :]
            ch["qm2"] = jnp.concatenate([q_e * masks[0], q_e * masks[1]], axis=0).astype(BF16)
            ch["k_e"] = (k * jnp.exp(-b)).astype(BF16)
            ch["k_d"] = (k * jnp.exp(b_last - b)).astype(BF16)
            ch["decay"] = jnp.sum(jnp.where(eye, jnp.exp(b_last), 0.0), axis=1, keepdims=True)
        for ch in chains:
            ch["a2"] = _dot_nt(ch["qm2"], ch["k_e"])
            ch["kv"] = _dot_tn(ch["k_d"], gv_ref[0, ch["rows"], :])
        for d in range(2):
            st_ref = dirs[d][3]
            st = st_ref[...]
            for ch in chains:
                if ch["d"] == d:
                    ch["st_prev"] = st.astype(BF16)
                    st = ch["decay"] * st + ch["kv"]
            st_ref[...] = st
        for ch in chains:
            d = ch["d"]
            a2 = jnp.where(keep2[d], ch["a2"], 0.0).astype(BF16)
            o2 = _dot(jnp.concatenate([ch["qm2"], a2], axis=1),
                      jnp.concatenate([ch["st_prev"], gv_ref[0, ch["rows"], :]], axis=0))
            for hh in range(2):
                rows_h = slice(GLA_CHUNK * hh, GLA_CHUNK * (hh + 1))
                cols = slice(GLA_DV * hh, GLA_DV * (hh + 1))
                dirs[d][4][ch["rows"], cols] = o2[rows_h, cols]
        return carry

    lax.fori_loop(0, n_chunks // GLA_GROUP, step, 0)

    def finish(i, carry):
        rows = pl.ds(pl.multiple_of(i * GLA_FINAL_ROWS, GLA_FINAL_ROWS), GLA_FINAL_ROWS)
        for hh in range(2):
            cols = slice(GLA_DV * hh, GLA_DV * (hh + 1))
            o = _rms(ofwd_ref[rows, cols] + obwd_ref[rows, cols], gain_ref[...])
            o_ref[0, rows, cols] = (o * gr_ref[0, rows, cols]).astype(BF16)
        return carry

    lax.fori_loop(0, seq // GLA_FINAL_ROWS, finish, 0)


def _gla(gq, gk, gv, gf, gb, gr, s0, gain):
    b, n, _ = gq.shape
    assert (n // GLA_CHUNK) % GLA_GROUP == 0 and n % GLA_FINAL_ROWS == 0
    n_pairs = GLA_HEADS // 2
    key_spec = pl.BlockSpec((1, n, LANES), lambda bi, pi: (bi, 0, pi))
    val_spec = pl.BlockSpec((1, n, 2 * GLA_DV), lambda bi, pi: (bi, 0, pi))
    return pl.pallas_call(
        _gla_kernel,
        grid=(b, n_pairs),
        in_specs=[key_spec, key_spec, val_spec, key_spec, key_spec, val_spec,
                  pl.BlockSpec((1, LANES, 2 * GLA_DV), lambda bi, pi: (pi, 0, 0)),
                  pl.BlockSpec((1, GLA_DV), lambda bi, pi: (0, 0))],
        out_specs=val_spec,
        out_shape=jax.ShapeDtypeStruct((b, n, GLA_VAL_WIDTH), BF16),
        scratch_shapes=[pltpu.VMEM((n, 2 * GLA_DV), F32), pltpu.VMEM((n, 2 * GLA_DV), F32),
                        pltpu.VMEM((LANES, 2 * GLA_DV), F32), pltpu.VMEM((LANES, 2 * GLA_DV), F32)],
        compiler_params=pltpu.CompilerParams(dimension_semantics=("arbitrary", "arbitrary"),
                                             vmem_limit_bytes=VMEM_LIMIT),
        name="gla",
    )(gq, gk, gv, gf, gb, gr, s0, gain)


def _out_ffn_kernel(x_ref, na_ref, gla_ref, wout_ref, gffn_ref, wg_ref, wu_ref, wd_ref, gfin_ref, o_ref):
    subs = [slice(r, r + SUB_ROWS) for r in range(0, x_ref.shape[0], SUB_ROWS)]
    mixes = [_dot(na_ref[rows, :], wout_ref[0:NA_WIDTH, :]) + _dot(gla_ref[rows, :], wout_ref[NA_WIDTH:, :])
             for rows in subs]
    hs = [x_ref[rows, :] + mix for rows, mix in zip(subs, mixes)]
    hns = [_rms(h, gffn_ref[...]).astype(BF16) for h in hs]
    acts = [(_silu(_dot(hn, wg_ref[...])) * _dot(hn, wu_ref[...])).astype(BF16) for hn in hns]
    outs = [h + _dot(act, wd_ref[...]) for h, act in zip(hs, acts)]
    for rows, out in zip(subs, outs):
        o_ref[rows, :] = _rms(out, gfin_ref[...])


def _out_ffn(x2d, na2d, gla2d, w_out, g_ffn, w_gate, w_up, w_down, g_fin, tm):
    m, d = x2d.shape
    const = lambda shape: pl.BlockSpec(shape, lambda i: (0, 0))
    row = lambda width: pl.BlockSpec((tm, width), lambda i: (i, 0))
    return pl.pallas_call(
        _out_ffn_kernel,
        grid=(m // tm,),
        in_specs=[row(d), row(NA_WIDTH), row(GLA_VAL_WIDTH), const(w_out.shape), const((1, d)),
                  const(w_gate.shape), const(w_up.shape), const(w_down.shape), const((1, d))],
        out_specs=row(d),
        out_shape=jax.ShapeDtypeStruct((m, d), F32),
        compiler_params=pltpu.CompilerParams(dimension_semantics=("arbitrary",),
                                             vmem_limit_bytes=56 * 1024 * 1024),
        name="out_ffn",
    )(x2d, na2d, gla2d, w_out, g_ffn, w_gate, w_up, w_down, g_fin)


def _na_bias_pairs(rpb):
    cols = np.arange(GRID_W)
    col_start = np.clip(cols - NA_KW // 2, 0, GRID_W - NA_KW)
    valid = (cols[None, :] >= col_start[:, None]) & (cols[None, :] < col_start[:, None] + NA_KW)
    dc = np.clip(cols[None, :] - cols[:, None], -(NA_KW - 1), NA_KW - 1) + (NA_KW - 1)
    onehot = (dc[None] == np.arange(2 * NA_KW - 1)[:, None, None]).astype(np.float32)
    expanded = jnp.einsum("hro,oqk->hrqk", rpb, onehot, precision=lax.Precision.HIGHEST)
    tiles = jnp.where(valid[None, None], expanded * LOG2_E, NEG_INF).astype(F32)
    two_rows = jnp.concatenate([tiles[:, :-1], tiles[:, 1:]], axis=-1)
    n_dr = two_rows.shape[1]
    stacked = two_rows.reshape(NA_HEADS // 2, 2, n_dr, GRID_W, LANES).transpose(0, 2, 1, 3, 4)
    return stacked.reshape(NA_HEADS // 2, n_dr, 2 * GRID_W, LANES)


def _na_meta_bias_pairs(meta_bias):
    return jnp.repeat(meta_bias.reshape(NA_HEADS // 2, 2, N_META) * LOG2_E, GRID_W, axis=1)


def kernel(x, meta_tokens, norm_mix_gain, w_in, rpb, meta_bias, w_gate_up_fwd, b_gate_fwd, w_gate_up_bwd, b_gate_bwd, gla_norm_gain, w_out, norm_ffn_gain, w_ffn_gate, w_ffn_up, w_ffn_down, norm_final_gain):
    bsz, seq, d = x.shape
    assert w_in.shape[0] == 1, "single-layer block: meta-token outputs are never consumed"
    assert w_in.shape[2] == IN_WIDTH and seq % GRID_W == 0 and GLA_CHUNK == GRID_W

    w_pad = jnp.concatenate([w_in[0], jnp.zeros((d, IN_WIDTH_PAD - IN_WIDTH), F32)], axis=1).astype(BF16)
    zeros_up = jnp.zeros((GATE_SLAB - 2 * GLA_GATE_RANK, GLA_KEY_WIDTH), F32)
    wupf = jnp.concatenate([w_gate_up_fwd[0], jnp.zeros_like(w_gate_up_bwd[0]), zeros_up], axis=0).astype(BF16)
    wupb = jnp.concatenate([jnp.zeros_like(w_gate_up_fwd[0]), w_gate_up_bwd[0], zeros_up], axis=0).astype(BF16)
    bgf = b_gate_fwd[0][None, :]
    bgb = b_gate_bwd[0][None, :]
    gain_mix = norm_mix_gain[0][None, :]

    x2d = x.reshape(bsz * seq, d)
    proj_args = (gain_mix, w_pad, wupf, bgf, wupb, bgb)
    naq, nak, nav, gq, gk, gv, gr, gf, gb = _in_proj(x2d, *proj_args, tm=1024)
    _, km, vm, _, gk_m, gv_m, _, gf_m, _ = _in_proj(meta_tokens, *proj_args, tm=N_META)

    s0 = _gla_init(gk_m, gv_m, gf_m)

    per_batch = lambda a: a.reshape(bsz, seq, a.shape[-1])
    na_out = _na(per_batch(naq), per_batch(nak), per_batch(nav), km, vm,
                 _na_bias_pairs(rpb[0]), _na_meta_bias_pairs(meta_bias[0]))
    gla_out = _gla(per_batch(gq), per_batch(gk), per_batch(gv), per_batch(gf), per_batch(gb),
                   per_batch(gr), s0, gla_norm_gain[0][None, :])

    out = _out_ffn(x2d, na_out.reshape(bsz * seq, NA_WIDTH), gla_out.reshape(bsz * seq, GLA_VAL_WIDTH),
                   w_out[0].astype(BF16), norm_ffn_gain[0][None, :], w_ffn_gate[0].astype(BF16),
                   w_ffn_up[0].astype(BF16), w_ffn_down[0].astype(BF16), norm_final_gain[None, :], tm=512)
    return out.reshape(bsz, seq, d)
```

```python
import functools

import jax
import jax.numpy as jnp
import numpy as np
from jax import lax
from jax.experimental import pallas as pl
from jax.experimental.pallas import tpu as pltpu

F32 = jnp.float32
BF16 = jnp.bfloat16

N_META = 16
GRID_W = 64
NA_HEADS = 8
NA_HEAD_DIM = 64
NA_WIDTH = NA_HEADS * NA_HEAD_DIM
NA_KH_MAX = 8
NA_KW = 16
GLA_HEADS = 4
GLA_DK = 64
GLA_DV = 128
GLA_KEY_WIDTH = GLA_HEADS * GLA_DK
GLA_VAL_WIDTH = GLA_HEADS * GLA_DV
GLA_GATE_RANK = 16
GLA_GATE_TAU = 16.0
GLA_CHUNK = 64
RMS_EPS = 1e-6
NEG_INF = -1e30
LOG2_E = 1.4426950408889634
NA_Q_SCALE = NA_HEAD_DIM ** -0.5 * LOG2_E

LANES = 128
V7X_VMEM_BYTES = 64 * 1024 * 1024
VMEM_LIMIT = V7X_VMEM_BYTES * 3 // 4
VMEM_LIMIT_FFN = V7X_VMEM_BYTES * 7 // 8
NA_ROWS_PER_STEP = 4
IN_PROJ_SUB_ROWS = 512
FFN_SUB_ROWS = 256

_OFF_NA_Q = 0
_OFF_NA_K = _OFF_NA_Q + NA_WIDTH
_OFF_NA_V = _OFF_NA_K + NA_WIDTH
_OFF_G_Q = _OFF_NA_V + NA_WIDTH
_OFF_G_K = _OFF_G_Q + GLA_KEY_WIDTH
_OFF_G_V = _OFF_G_K + GLA_KEY_WIDTH
_OFF_G_R = _OFF_G_V + GLA_VAL_WIDTH
_OFF_GATE = _OFF_G_R + GLA_VAL_WIDTH
IN_WIDTH = _OFF_GATE + 2 * GLA_GATE_RANK


def _rms(x, gain):
    ms = jnp.mean(x * x, axis=-1, keepdims=True)
    return x * lax.rsqrt(ms + RMS_EPS) * gain


def _silu(x):
    return x * (1.0 / (1.0 + jnp.exp(-x)))


def _log_sigmoid(x):
    return jnp.minimum(x, 0.0) - jnp.log(1.0 + jnp.exp(-jnp.abs(x)))


def _dot(a, b):
    return jnp.dot(a, b, preferred_element_type=F32)


def _dot_nt(a, b):
    return lax.dot_general(a, b, (((1,), (1,)), ((), ())), preferred_element_type=F32)


def _dot_tn(a, b):
    return lax.dot_general(a, b, (((0,), (0,)), ((), ())), preferred_element_type=F32)


def _head_masks(rows, dtype):
    lane = lax.broadcasted_iota(jnp.int32, (rows, LANES), 1)
    lo = jnp.where(lane < LANES // 2, 1.0, 0.0).astype(dtype)
    hi = jnp.where(lane >= LANES // 2, 1.0, 0.0).astype(dtype)
    return lo, hi


def _in_proj_kernel(x_ref, gain_ref, w_ref, wupf_ref, bf_ref, wupb_ref, bb_ref,
                    naq_ref, nak_ref, nav_ref, gq_ref, gk_ref, gv_ref, gr_ref, gf_ref, gb_ref):
    tm = x_ref.shape[0]
    sub = min(tm, IN_PROJ_SUB_ROWS)
    subs = [slice(r, r + sub) for r in range(0, tm, sub)]
    xns = [_rms(x_ref[rows, :], gain_ref[...]).astype(BF16) for rows in subs]
    for rows, xn in zip(subs, xns):

        def proj(lo, width):
            return _dot(xn, w_ref[:, lo:lo + width])

        half = GLA_VAL_WIDTH // 2
        low = proj(_OFF_GATE, 2 * GLA_GATE_RANK).astype(BF16)
        naq_ref[rows, :] = (proj(_OFF_NA_Q, NA_WIDTH) * NA_Q_SCALE).astype(BF16)
        gr_ref[rows, 0:half] = _silu(proj(_OFF_G_R, half))
        gq_ref[rows, :] = proj(_OFF_G_Q, GLA_KEY_WIDTH) * (GLA_DK ** -0.5)
        gf_ref[rows, :] = _log_sigmoid(_dot(low, wupf_ref[...]) + bf_ref[...]) * (1.0 / GLA_GATE_TAU)
        gv_ref[rows, :] = proj(_OFF_G_V, GLA_VAL_WIDTH).astype(BF16)
        gr_ref[rows, half:] = _silu(proj(_OFF_G_R + half, half))
        gk_ref[rows, :] = proj(_OFF_G_K, GLA_KEY_WIDTH)
        gb_ref[rows, :] = _log_sigmoid(_dot(low, wupb_ref[...]) + bb_ref[...]) * (1.0 / GLA_GATE_TAU)
        nak_ref[rows, :] = proj(_OFF_NA_K, NA_WIDTH).astype(BF16)
        nav_ref[rows, :] = proj(_OFF_NA_V, NA_WIDTH).astype(BF16)


def _in_proj(x2d, gain, w, wupf, bgf, wupb, bgb, tm):
    m, d = x2d.shape
    const = lambda shape: pl.BlockSpec(shape, lambda i: (0, 0))
    row = lambda width: pl.BlockSpec((tm, width), lambda i: (i, 0))
    widths = (NA_WIDTH, NA_WIDTH, NA_WIDTH, GLA_KEY_WIDTH, GLA_KEY_WIDTH,
              GLA_VAL_WIDTH, GLA_VAL_WIDTH, GLA_KEY_WIDTH, GLA_KEY_WIDTH)
    dtypes = (BF16, BF16, BF16, F32, F32, BF16, F32, F32, F32)
    return pl.pallas_call(
        _in_proj_kernel,
        grid=(m // tm,),
        in_specs=[row(d), const((1, d)), const(w.shape), const(wupf.shape), const(bgf.shape),
                  const(wupb.shape), const(bgb.shape)],
        out_specs=[row(w) for w in widths],
        out_shape=[jax.ShapeDtypeStruct((m, w), dt) for w, dt in zip(widths, dtypes)],
        compiler_params=pltpu.CompilerParams(dimension_semantics=("arbitrary",),
                                             vmem_limit_bytes=VMEM_LIMIT),
        name="in_proj",
    )(x2d, gain, w, wupf, bgf, wupb, bgb)


def _gla_init_kernel(gk_ref, gv_ref, gf_ref, st_ref):
    n = gk_ref.shape[0]
    r = lax.broadcasted_iota(jnp.int32, (n, n), 0)
    c = lax.broadcasted_iota(jnp.int32, (n, n), 1)
    later = jnp.where(c > r, 1.0, 0.0).astype(BF16)
    g = gf_ref[...]
    g_hi = g.astype(BF16)
    g_lo = (g - g_hi.astype(F32)).astype(BF16)
    tail = _dot(later, g_hi) + _dot(later, g_lo)
    k_d = (gk_ref[...] * jnp.exp(tail)).astype(BF16)
    for p in range(GLA_HEADS // 2):
        st_ref[p] = _dot_tn(k_d[:, LANES * p:LANES * (p + 1)], gv_ref[:, 2 * GLA_DV * p:2 * GLA_DV * (p + 1)])


def _gla_init(gk_m, gv_m, gf_m):
    return pl.pallas_call(
        _gla_init_kernel,
        out_shape=jax.ShapeDtypeStruct((GLA_HEADS // 2, LANES, 2 * GLA_DV), F32),
        name="gla_init",
    )(gk_m, gv_m, gf_m)


def _na_kernel(q_ref, k_ref, v_ref, km_ref, vm_ref, bias_ref, mb_ref, o_ref, sw_ref, sm_ref, *, n_rows):
    kh = min(NA_KH_MAX, n_rows)
    n_pairs = NA_HEADS // 2
    pair_lanes = [slice(LANES * p, LANES * (p + 1)) for p in range(n_pairs)]
    masks = _head_masks(GRID_W, BF16)
    sel_lo = lax.broadcasted_iota(jnp.int32, (GRID_W, LANES), 1) < LANES // 2

    def window_start(i):
        return jnp.clip(i - kh // 2, 0, n_rows - kh)

    def row_scores(i, slot):
        qoff = pl.multiple_of(i * GRID_W, GRID_W)
        koff = pl.multiple_of(window_start(i) * GRID_W, GRID_W)
        for p in range(n_pairs):
            q_pair = q_ref[0, pl.ds(qoff, GRID_W), pair_lanes[p]]
            q2 = jnp.concatenate([q_pair * masks[0], q_pair * masks[1]], axis=0)
            k_win = k_ref[0, pl.ds(koff, kh * GRID_W), pair_lanes[p]]
            sw_ref[slot, p] = _dot_nt(q2, k_win)
            sm_ref[slot, p] = _dot_nt(q2, km_ref[:, pair_lanes[p]])

    def row_finish(i, slot):
        s = window_start(i)
        dr0 = s - i + (NA_KH_MAX - 1)
        qoff = pl.multiple_of(i * GRID_W, GRID_W)
        koff = pl.multiple_of(s * GRID_W, GRID_W)
        biased = []
        for p in range(n_pairs):
            s_meta = sm_ref[slot, p] + mb_ref[p]
            chunks = [sw_ref[slot, p, :, LANES * t:LANES * (t + 1)]
                      + jnp.concatenate([bias_ref[2 * p, dr0 + 2 * t], bias_ref[2 * p + 1, dr0 + 2 * t]], axis=0)
                      for t in range(kh * GRID_W // LANES)]
            m_el = functools.reduce(jnp.maximum, chunks)
            m = jnp.maximum(jnp.max(m_el, axis=-1, keepdims=True),
                            jnp.max(s_meta, axis=-1, keepdims=True))
            biased.append((chunks, s_meta, m))
        probs = []
        for p in range(n_pairs):
            chunks, s_meta, m = biased[p]
            p_chunks = [jnp.exp2(c - m) for c in chunks]
            p_meta = jnp.exp2(s_meta - m)
            l = (jnp.sum(functools.reduce(jnp.add, p_chunks), axis=-1, keepdims=True)
                 + jnp.sum(p_meta, axis=-1, keepdims=True))
            probs.append((jnp.concatenate(p_chunks, axis=1).astype(BF16), p_meta.astype(BF16), l))
        pair_outs = []
        for p in range(n_pairs):
            p_win, p_meta, l = probs[p]
            v_win = v_ref[0, pl.ds(koff, kh * GRID_W), pair_lanes[p]]
            o2 = (_dot(p_win, v_win) + _dot(p_meta, vm_ref[:, pair_lanes[p]])) / l
            pair_outs.append(jnp.where(sel_lo, o2[:GRID_W], o2[GRID_W:]).astype(BF16))
        o_ref[0, pl.ds(qoff, GRID_W), :] = jnp.concatenate(pair_outs, axis=1)

    def rows_step(j, carry):
        a = NA_ROWS_PER_STEP * j
        for u in range(NA_ROWS_PER_STEP):
            row_scores(jnp.minimum(a + u + 1, n_rows - 1), (u + 1) % 2)
            row_finish(a + u, u % 2)
        return carry

    row_scores(0, 0)
    lax.fori_loop(0, n_rows // NA_ROWS_PER_STEP, rows_step, 0)


def _na(q, k, v, k_meta, v_meta, bias_pairs, meta_bias):
    b, n, w = q.shape
    n_rows = n // GRID_W
    assert n_rows % NA_ROWS_PER_STEP == 0
    kh = min(NA_KH_MAX, n_rows)
    kern = functools.partial(_na_kernel, n_rows=n_rows)
    batch_spec = pl.BlockSpec((1, n, w), lambda bi: (bi, 0, 0))
    whole = lambda a: pl.BlockSpec(a.shape, lambda bi: (0,) * a.ndim)
    return pl.pallas_call(
        kern,
        grid=(b,),
        in_specs=[batch_spec, batch_spec, batch_spec, whole(k_meta), whole(v_meta), whole(bias_pairs),
                  whole(meta_bias)],
        out_specs=batch_spec,
        out_shape=jax.ShapeDtypeStruct((b, n, w), BF16),
        scratch_shapes=[pltpu.VMEM((2, NA_HEADS // 2, 2 * GRID_W, kh * GRID_W), F32),
                        pltpu.VMEM((2, NA_HEADS // 2, 2 * GRID_W, N_META), F32)],
        compiler_params=pltpu.CompilerParams(dimension_semantics=("arbitrary",),
                                             vmem_limit_bytes=VMEM_LIMIT),
        name="na",
    )(q, k, v, k_meta, v_meta, bias_pairs, meta_bias)


GLA_GROUP = 8
GLA_FINAL_ROWS = 1024


def _gla_kernel(gq_ref, gk_ref, gv_ref, gf_ref, gb_ref, gr_ref, s0_ref, gain_ref, o_ref,
                ofwd_ref, obwd_ref, stf_ref, stb_ref):
    seq = gq_ref.shape[1]
    n_chunks = seq // GLA_CHUNK
    r = lax.broadcasted_iota(jnp.int32, (GLA_CHUNK, GLA_CHUNK), 0)
    c = lax.broadcasted_iota(jnp.int32, (GLA_CHUNK, GLA_CHUNK), 1)
    masks = _head_masks(GLA_CHUNK, F32)
    dirs = ((gf_ref, c <= r, GLA_CHUNK - 1, stf_ref, ofwd_ref),
            (gb_ref, c >= r, 0, stb_ref, obwd_ref))
    tris = [jnp.where(d[1], 1.0, 0.0).astype(BF16) for d in dirs]
    keep2 = [jnp.concatenate([d[1], d[1]], axis=0) for d in dirs]
    eye = (lax.broadcasted_iota(jnp.int32, (LANES, LANES), 0)
           == lax.broadcasted_iota(jnp.int32, (LANES, LANES), 1))

    stf_ref[...] = s0_ref[0]
    stb_ref[...] = jnp.zeros_like(stb_ref)

    def step(t, carry):
        chains = []
        for d in range(2):
            for j in range(GLA_GROUP):
                idx = t * GLA_GROUP + j
                n = idx if d == 0 else n_chunks - 1 - idx
                chains.append({"d": d, "rows": pl.ds(pl.multiple_of(n * GLA_CHUNK, GLA_CHUNK), GLA_CHUNK)})
        for ch in chains:
            g = dirs[ch["d"]][0][0, ch["rows"], :]
            g_hi = g.astype(BF16)
            g_lo = (g - g_hi.astype(F32)).astype(BF16)
            ch["b2"] = _dot(tris[ch["d"]], jnp.concatenate([g_hi, g_lo], axis=1))
        for ch in chains:
            last = dirs[ch["d"]][2]
            b = ch["b2"][:, :LANES] + ch["b2"][:, LANES:]
            b_last = b[last:last + 1, :]
            q_e = gq_ref[0, ch["rows"], :] * jnp.exp(b)
            k = gk_ref[0, ch["rows"], :]
            ch["qm2"] = jnp.concatenate([q_e * masks[0], q_e * masks[1]], axis=0).astype(BF16)
            ch["k_e"] = (k * jnp.exp(-b)).astype(BF16)
            ch["k_d"] = (k * jnp.exp(b_last - b)).astype(BF16)
            ch["decay"] = jnp.sum(jnp.where(eye, jnp.exp(b_last), 0.0), axis=1, keepdims=True)
        for ch in chains:
            ch["a2"] = _dot_nt(ch["qm2"], ch["k_e"])
            ch["kv"] = _dot_tn(ch["k_d"], gv_ref[0, ch["rows"], :])
        for d in range(2):
            st_ref = dirs[d][3]
            st = st_ref[...]
            for ch in chains:
                if ch["d"] == d:
                    ch["st_prev"] = st.astype(BF16)
                    st = ch["decay"] * st + ch["kv"]
            st_ref[...] = st
        for ch in chains:
            d = ch["d"]
            a2 = jnp.where(keep2[d], ch["a2"], 0.0).astype(BF16)
            o2 = _dot(jnp.concatenate([ch["qm2"], a2], axis=1),
                      jnp.concatenate([ch["st_prev"], gv_ref[0, ch["rows"], :]], axis=0))
            for hh in range(2):
                rows_h = slice(GLA_CHUNK * hh, GLA_CHUNK * (hh + 1))
                cols = slice(GLA_DV * hh, GLA_DV * (hh + 1))
                dirs[d][4][ch["rows"], cols] = o2[rows_h, cols]
        return carry

    lax.fori_loop(0, n_chunks // GLA_GROUP, step, 0)

    def finish(i, carry):
        rows = pl.ds(pl.multiple_of(i * GLA_FINAL_ROWS, GLA_FINAL_ROWS), GLA_FINAL_ROWS)
        for hh in range(2):
            cols = slice(GLA_DV * hh, GLA_DV * (hh + 1))
            o = _rms(ofwd_ref[rows, cols] + obwd_ref[rows, cols], gain_ref[...])
            o_ref[0, rows, cols] = (o * gr_ref[0, rows, cols]).astype(BF16)
        return carry

    lax.fori_loop(0, seq // GLA_FINAL_ROWS, finish, 0)


def _gla(gq, gk, gv, gf, gb, gr, s0, gain):
    b, n, _ = gq.shape
    assert (n // GLA_CHUNK) % GLA_GROUP == 0 and n % GLA_FINAL_ROWS == 0
    n_pairs = GLA_HEADS // 2
    key_spec = pl.BlockSpec((1, n, LANES), lambda bi, pi: (bi, 0, pi))
    val_spec = pl.BlockSpec((1, n, 2 * GLA_DV), lambda bi, pi: (bi, 0, pi))
    return pl.pallas_call(
        _gla_kernel,
        grid=(b, n_pairs),
        in_specs=[key_spec, key_spec, val_spec, key_spec, key_spec, val_spec,
                  pl.BlockSpec((1, LANES, 2 * GLA_DV), lambda bi, pi: (pi, 0, 0)),
                  pl.BlockSpec((1, GLA_DV), lambda bi, pi: (0, 0))],
        out_specs=val_spec,
        out_shape=jax.ShapeDtypeStruct((b, n, GLA_VAL_WIDTH), BF16),
        scratch_shapes=[pltpu.VMEM((n, 2 * GLA_DV), F32), pltpu.VMEM((n, 2 * GLA_DV), F32),
                        pltpu.VMEM((LANES, 2 * GLA_DV), F32), pltpu.VMEM((LANES, 2 * GLA_DV), F32)],
        compiler_params=pltpu.CompilerParams(dimension_semantics=("arbitrary", "arbitrary"),
                                             vmem_limit_bytes=VMEM_LIMIT),
        name="gla",
    )(gq, gk, gv, gf, gb, gr, s0, gain)


def _out_ffn_kernel(x_ref, na_ref, gla_ref, wout_ref, gffn_ref, wg_ref, wu_ref, wd_ref, gfin_ref, o_ref):
    subs = [slice(r, r + FFN_SUB_ROWS) for r in range(0, x_ref.shape[0], FFN_SUB_ROWS)]
    mixes = [_dot(na_ref[rows, :], wout_ref[0:NA_WIDTH, :]) + _dot(gla_ref[rows, :], wout_ref[NA_WIDTH:, :])
             for rows in subs]
    hs = [x_ref[rows, :] + mix for rows, mix in zip(subs, mixes)]
    hns = [_rms(h, gffn_ref[...]).astype(BF16) for h in hs]
    acts = [(_silu(_dot(hn, wg_ref[...])) * _dot(hn, wu_ref[...])).astype(BF16) for hn in hns]
    outs = [h + _dot(act, wd_ref[...]) for h, act in zip(hs, acts)]
    for rows, out in zip(subs, outs):
        o_ref[rows, :] = _rms(out, gfin_ref[...])


def _out_ffn(x2d, na2d, gla2d, w_out, g_ffn, w_gate, w_up, w_down, g_fin, tm):
    m, d = x2d.shape
    const = lambda shape: pl.BlockSpec(shape, lambda i: (0, 0))
    row = lambda width: pl.BlockSpec((tm, width), lambda i: (i, 0))
    return pl.pallas_call(
        _out_ffn_kernel,
        grid=(m // tm,),
        in_specs=[row(d), row(NA_WIDTH), row(GLA_VAL_WIDTH), const(w_out.shape), const((1, d)),
                  const(w_gate.shape), const(w_up.shape), const(w_down.shape), const((1, d))],
        out_specs=row(d),
        out_shape=jax.ShapeDtypeStruct((m, d), F32),
        compiler_params=pltpu.CompilerParams(dimension_semantics=("arbitrary",),
                                             vmem_limit_bytes=VMEM_LIMIT_FFN),
        name="out_ffn",
    )(x2d, na2d, gla2d, w_out, g_ffn, w_gate, w_up, w_down, g_fin)


def _na_bias_pairs(rpb):
    cols = np.arange(GRID_W)
    col_start = np.clip(cols - NA_KW // 2, 0, GRID_W - NA_KW)
    valid = (cols[None, :] >= col_start[:, None]) & (cols[None, :] < col_start[:, None] + NA_KW)
    dc = np.clip(cols[None, :] - cols[:, None], -(NA_KW - 1), NA_KW - 1) + (NA_KW - 1)
    onehot = (dc[None] == np.arange(2 * NA_KW - 1)[:, None, None]).astype(np.float32)
    expanded = jnp.einsum("hro,oqk->hrqk", rpb, onehot, precision=lax.Precision.HIGHEST)
    tiles = jnp.where(valid[None, None], expanded * LOG2_E, NEG_INF).astype(F32)
    return jnp.concatenate([tiles[:, :-1], tiles[:, 1:]], axis=-1)


def _na_meta_bias_pairs(meta_bias):
    return jnp.repeat(meta_bias.reshape(NA_HEADS // 2, 2, N_META) * LOG2_E, GRID_W, axis=1)


def kernel(x, meta_tokens, norm_mix_gain, w_in, rpb, meta_bias, w_gate_up_fwd, b_gate_fwd, w_gate_up_bwd, b_gate_bwd, gla_norm_gain, w_out, norm_ffn_gain, w_ffn_gate, w_ffn_up, w_ffn_down, norm_final_gain):
    bsz, seq, d = x.shape
    assert w_in.shape[0] == 1, "single-layer block: meta-token outputs are never consumed"
    assert w_in.shape[2] == IN_WIDTH and seq % GRID_W == 0 and GLA_CHUNK == GRID_W

    w_bf = w_in[0].astype(BF16)
    wupf = jnp.concatenate([w_gate_up_fwd[0], jnp.zeros_like(w_gate_up_bwd[0])], axis=0).astype(BF16)
    wupb = jnp.concatenate([jnp.zeros_like(w_gate_up_fwd[0]), w_gate_up_bwd[0]], axis=0).astype(BF16)
    bgf = b_gate_fwd[0][None, :]
    bgb = b_gate_bwd[0][None, :]
    gain_mix = norm_mix_gain[0][None, :]

    x2d = x.reshape(bsz * seq, d)
    proj_args = (gain_mix, w_bf, wupf, bgf, wupb, bgb)
    naq, nak, nav, gq, gk, gv, gr, gf, gb = _in_proj(x2d, *proj_args, tm=1024)
    _, km, vm, _, gk_m, gv_m, _, gf_m, _ = _in_proj(meta_tokens, *proj_args, tm=N_META)

    s0 = _gla_init(gk_m, gv_m, gf_m)

    per_batch = lambda a: a.reshape(bsz, seq, a.shape[-1])
    na_out = _na(per_batch(naq), per_batch(nak), per_batch(nav), km, vm,
                 _na_bias_pairs(rpb[0]), _na_meta_bias_pairs(meta_bias[0]))
    gla_out = _gla(per_batch(gq), per_batch(gk), per_batch(gv), per_batch(gf), per_batch(gb),
                   per_batch(gr), s0, gla_norm_gain[0][None, :])

    out = _out_ffn(x2d, na_out.reshape(bsz * seq, NA_WIDTH), gla_out.reshape(bsz * seq, GLA_VAL_WIDTH),
                   w_out[0].astype(BF16), norm_ffn_gain[0][None, :], w_ffn_gate[0].astype(BF16),
                   w_ffn_up[0].astype(BF16), w_ffn_down[0].astype(BF16), norm_final_gain[None, :], tm=512)
    return out.reshape(bsz, seq, d)
```

```python
import functools

import jax
import jax.numpy as jnp
import numpy as np
from jax import lax
from jax.experimental import pallas as pl
from jax.experimental.pallas import tpu as pltpu

F32 = jnp.float32
BF16 = jnp.bfloat16

N_META = 16
GRID_W = 64
NA_HEADS = 8
NA_HEAD_DIM = 64
NA_WIDTH = NA_HEADS * NA_HEAD_DIM
NA_KH_MAX = 8
NA_KW = 16
GLA_HEADS = 4
GLA_DK = 64
GLA_DV = 128
GLA_KEY_WIDTH = GLA_HEADS * GLA_DK
GLA_VAL_WIDTH = GLA_HEADS * GLA_DV
GLA_GATE_RANK = 16
GLA_GATE_TAU = 16.0
GLA_CHUNK = 64
RMS_EPS = 1e-6
NEG_INF = -1e30
LOG2_E = 1.4426950408889634
NA_Q_SCALE = NA_HEAD_DIM ** -0.5 * LOG2_E

LANES = 128
V7X_VMEM_BYTES = 64 * 1024 * 1024
VMEM_LIMIT = V7X_VMEM_BYTES * 3 // 4
VMEM_LIMIT_FFN = V7X_VMEM_BYTES * 7 // 8
NA_ROWS_PER_STEP = 4
IN_PROJ_SUB_ROWS = 512
FFN_SUB_ROWS = 256

_OFF_NA_Q = 0
_OFF_NA_K = _OFF_NA_Q + NA_WIDTH
_OFF_NA_V = _OFF_NA_K + NA_WIDTH
_OFF_G_Q = _OFF_NA_V + NA_WIDTH
_OFF_G_K = _OFF_G_Q + GLA_KEY_WIDTH
_OFF_G_V = _OFF_G_K + GLA_KEY_WIDTH
_OFF_G_R = _OFF_G_V + GLA_VAL_WIDTH
_OFF_GATE = _OFF_G_R + GLA_VAL_WIDTH
IN_WIDTH = _OFF_GATE + 2 * GLA_GATE_RANK


def _rms(x, gain):
    ms = jnp.mean(x * x, axis=-1, keepdims=True)
    return x * lax.rsqrt(ms + RMS_EPS) * gain


def _silu(x):
    return x * (1.0 / (1.0 + jnp.exp(-x)))


def _log_sigmoid(x):
    return jnp.minimum(x, 0.0) - jnp.log(1.0 + jnp.exp(-jnp.abs(x)))


def _dot(a, b):
    return jnp.dot(a, b, preferred_element_type=F32)


def _dot_nt(a, b):
    return lax.dot_general(a, b, (((1,), (1,)), ((), ())), preferred_element_type=F32)


def _dot_tn(a, b):
    return lax.dot_general(a, b, (((0,), (0,)), ((), ())), preferred_element_type=F32)


def _head_masks(rows, dtype):
    lane = lax.broadcasted_iota(jnp.int32, (rows, LANES), 1)
    lo = jnp.where(lane < LANES // 2, 1.0, 0.0).astype(dtype)
    hi = jnp.where(lane >= LANES // 2, 1.0, 0.0).astype(dtype)
    return lo, hi


def _in_proj_kernel(x_ref, gain_ref, w_ref, wupf_ref, bf_ref, wupb_ref, bb_ref,
                    naq_ref, nak_ref, nav_ref, gq_ref, gk_ref, gv_ref, gr_ref, gf_ref, gb_ref):
    tm = x_ref.shape[0]
    sub = min(tm, IN_PROJ_SUB_ROWS)
    subs = [slice(r, r + sub) for r in range(0, tm, sub)]
    xns = [_rms(x_ref[rows, :], gain_ref[...]).astype(BF16) for rows in subs]
    for rows, xn in zip(subs, xns):

        def proj(lo, width):
            return _dot(xn, w_ref[:, lo:lo + width])

        half = GLA_VAL_WIDTH // 2
        low = proj(_OFF_GATE, 2 * GLA_GATE_RANK).astype(BF16)
        naq_ref[rows, :] = (proj(_OFF_NA_Q, NA_WIDTH) * NA_Q_SCALE).astype(BF16)
        gr_ref[rows, 0:half] = _silu(proj(_OFF_G_R, half))
        gq_ref[rows, :] = proj(_OFF_G_Q, GLA_KEY_WIDTH) * (GLA_DK ** -0.5)
        gf_ref[rows, :] = _log_sigmoid(_dot(low, wupf_ref[...]) + bf_ref[...]) * (1.0 / GLA_GATE_TAU)
        gv_ref[rows, :] = proj(_OFF_G_V, GLA_VAL_WIDTH).astype(BF16)
        gr_ref[rows, half:] = _silu(proj(_OFF_G_R + half, half))
        gk_ref[rows, :] = proj(_OFF_G_K, GLA_KEY_WIDTH)
        gb_ref[rows, :] = _log_sigmoid(_dot(low, wupb_ref[...]) + bb_ref[...]) * (1.0 / GLA_GATE_TAU)
        nak_ref[rows, :] = proj(_OFF_NA_K, NA_WIDTH).astype(BF16)
        nav_ref[rows, :] = proj(_OFF_NA_V, NA_WIDTH).astype(BF16)


def _in_proj(x2d, gain, w, wupf, bgf, wupb, bgb, tm):
    m, d = x2d.shape
    const = lambda shape: pl.BlockSpec(shape, lambda i: (0, 0))
    row = lambda width: pl.BlockSpec((tm, width), lambda i: (i, 0))
    widths = (NA_WIDTH, NA_WIDTH, NA_WIDTH, GLA_KEY_WIDTH, GLA_KEY_WIDTH,
              GLA_VAL_WIDTH, GLA_VAL_WIDTH, GLA_KEY_WIDTH, GLA_KEY_WIDTH)
    dtypes = (BF16, BF16, BF16, F32, F32, BF16, F32, F32, F32)
    return pl.pallas_call(
        _in_proj_kernel,
        grid=(m // tm,),
        in_specs=[row(d), const((1, d)), const(w.shape), const(wupf.shape), const(bgf.shape),
                  const(wupb.shape), const(bgb.shape)],
        out_specs=[row(w) for w in widths],
        out_shape=[jax.ShapeDtypeStruct((m, w), dt) for w, dt in zip(widths, dtypes)],
        compiler_params=pltpu.CompilerParams(dimension_semantics=("arbitrary",),
                                             vmem_limit_bytes=VMEM_LIMIT),
        name="in_proj",
    )(x2d, gain, w, wupf, bgf, wupb, bgb)


def _gla_init_kernel(gk_ref, gv_ref, gf_ref, st_ref):
    n = gk_ref.shape[0]
    r = lax.broadcasted_iota(jnp.int32, (n, n), 0)
    c = lax.broadcasted_iota(jnp.int32, (n, n), 1)
    later = jnp.where(c > r, 1.0, 0.0).astype(BF16)
    g = gf_ref[...]
    g_hi = g.astype(BF16)
    g_lo = (g - g_hi.astype(F32)).astype(BF16)
    tail = _dot(later, g_hi) + _dot(later, g_lo)
    k_d = (gk_ref[...] * jnp.exp(tail)).astype(BF16)
    for p in range(GLA_HEADS // 2):
        st_ref[p] = _dot_tn(k_d[:, LANES * p:LANES * (p + 1)], gv_ref[:, 2 * GLA_DV * p:2 * GLA_DV * (p + 1)])


def _gla_init(gk_m, gv_m, gf_m):
    return pl.pallas_call(
        _gla_init_kernel,
        out_shape=jax.ShapeDtypeStruct((GLA_HEADS // 2, LANES, 2 * GLA_DV), F32),
        name="gla_init",
    )(gk_m, gv_m, gf_m)


def _na_kernel(q_ref, k_ref, v_ref, km_ref, vm_ref, bias_ref, mb_ref, o_ref, sw_ref, sm_ref, *, n_rows):
    kh = min(NA_KH_MAX, n_rows)
    n_pairs = NA_HEADS // 2
    pair_lanes = [slice(LANES * p, LANES * (p + 1)) for p in range(n_pairs)]
    masks = _head_masks(GRID_W, BF16)
    sel_lo = lax.broadcasted_iota(jnp.int32, (GRID_W, LANES), 1) < LANES // 2

    def window_start(i):
        return jnp.clip(i - kh // 2, 0, n_rows - kh)

    def row_scores(i, slot):
        qoff = pl.multiple_of(i * GRID_W, GRID_W)
        koff = pl.multiple_of(window_start(i) * GRID_W, GRID_W)
        for p in range(n_pairs):
            q_pair = q_ref[0, pl.ds(qoff, GRID_W), pair_lanes[p]]
            q2 = jnp.concatenate([q_pair * masks[0], q_pair * masks[1]], axis=0)
            k_win = k_ref[0, pl.ds(koff, kh * GRID_W), pair_lanes[p]]
            sw_ref[slot, p] = _dot_nt(q2, k_win)
            sm_ref[slot, p] = _dot_nt(q2, km_ref[:, pair_lanes[p]])

    def row_finish(i, slot):
        s = window_start(i)
        dr0 = s - i + (NA_KH_MAX - 1)
        qoff = pl.multiple_of(i * GRID_W, GRID_W)
        koff = pl.multiple_of(s * GRID_W, GRID_W)
        biased = []
        for p in range(n_pairs):
            s_meta = sm_ref[slot, p] + mb_ref[p]
            chunks = [sw_ref[slot, p, :, LANES * t:LANES * (t + 1)]
                      + jnp.concatenate([bias_ref[2 * p, dr0 + 2 * t], bias_ref[2 * p + 1, dr0 + 2 * t]], axis=0)
                      for t in range(kh * GRID_W // LANES)]
            m_el = functools.reduce(jnp.maximum, chunks)
            m = jnp.maximum(jnp.max(m_el, axis=-1, keepdims=True),
                            jnp.max(s_meta, axis=-1, keepdims=True))
            biased.append((chunks, s_meta, m))
        probs = []
        for p in range(n_pairs):
            chunks, s_meta, m = biased[p]
            p_chunks = [jnp.exp2(c - m) for c in chunks]
            p_meta = jnp.exp2(s_meta - m)
            l = (jnp.sum(functools.reduce(jnp.add, p_chunks), axis=-1, keepdims=True)
                 + jnp.sum(p_meta, axis=-1, keepdims=True))
            probs.append((jnp.concatenate(p_chunks, axis=1).astype(BF16), p_meta.astype(BF16), l))
        pair_outs = []
        for p in range(n_pairs):
            p_win, p_meta, l = probs[p]
            v_win = v_ref[0, pl.ds(koff, kh * GRID_W), pair_lanes[p]]
            o2 = (_dot(p_win, v_win) + _dot(p_meta, vm_ref[:, pair_lanes[p]])) / l
            pair_outs.append(jnp.where(sel_lo, o2[:GRID_W], o2[GRID_W:]).astype(BF16))
        o_ref[0, pl.ds(qoff, GRID_W), :] = jnp.concatenate(pair_outs, axis=1)

    def rows_step(j, carry):
        a = NA_ROWS_PER_STEP * j
        for u in range(NA_ROWS_PER_STEP):
            row_scores(jnp.minimum(a + u + 1, n_rows - 1), (u + 1) % 2)
            row_finish(a + u, u % 2)
        return carry

    row_scores(0, 0)
    lax.fori_loop(0, n_rows // NA_ROWS_PER_STEP, rows_step, 0)


def _na(q, k, v, k_meta, v_meta, bias_pairs, meta_bias):
    b, n, w = q.shape
    n_rows = n // GRID_W
    assert n_rows % NA_ROWS_PER_STEP == 0
    kh = min(NA_KH_MAX, n_rows)
    kern = functools.partial(_na_kernel, n_rows=n_rows)
    batch_spec = pl.BlockSpec((1, n, w), lambda bi: (bi, 0, 0))
    whole = lambda a: pl.BlockSpec(a.shape, lambda bi: (0,) * a.ndim)
    return pl.pallas_call(
        kern,
        grid=(b,),
        in_specs=[batch_spec, batch_spec, batch_spec, whole(k_meta), whole(v_meta), whole(bias_pairs),
                  whole(meta_bias)],
        out_specs=batch_spec,
        out_shape=jax.ShapeDtypeStruct((b, n, w), BF16),
        scratch_shapes=[pltpu.VMEM((2, NA_HEADS // 2, 2 * GRID_W, kh * GRID_W), F32),
                        pltpu.VMEM((2, NA_HEADS // 2, 2 * GRID_W, N_META), F32)],
        compiler_params=pltpu.CompilerParams(dimension_semantics=("arbitrary",),
                                             vmem_limit_bytes=VMEM_LIMIT),
        name="na",
    )(q, k, v, k_meta, v_meta, bias_pairs, meta_bias)


GLA_GROUP = 8
GLA_FINAL_ROWS = 1024


def _gla_kernel(gq_ref, gk_ref, gv_ref, gf_ref, gb_ref, gr_ref, s0_ref, gain_ref, o_ref,
                ofwd_ref, obwd_ref, stf_ref, stb_ref):
    seq = gq_ref.shape[1]
    n_chunks = seq // GLA_CHUNK
    r = lax.broadcasted_iota(jnp.int32, (GLA_CHUNK, GLA_CHUNK), 0)
    c = lax.broadcasted_iota(jnp.int32, (GLA_CHUNK, GLA_CHUNK), 1)
    masks = _head_masks(GLA_CHUNK, F32)
    dirs = ((gf_ref, c <= r, GLA_CHUNK - 1, stf_ref, ofwd_ref),
            (gb_ref, c >= r, 0, stb_ref, obwd_ref))
    tris = [jnp.where(d[1], 1.0, 0.0).astype(BF16) for d in dirs]
    keep2 = [jnp.concatenate([d[1], d[1]], axis=0) for d in dirs]
    eye = (lax.broadcasted_iota(jnp.int32, (LANES, LANES), 0)
           == lax.broadcasted_iota(jnp.int32, (LANES, LANES), 1))

    stf_ref[...] = s0_ref[0]
    stb_ref[...] = jnp.zeros_like(stb_ref)

    def step(t, carry):
        chains = []
        for d in range(2):
            for j in range(GLA_GROUP):
                idx = t * GLA_GROUP + j
                n = idx if d == 0 else n_chunks - 1 - idx
                chains.append({"d": d, "rows": pl.ds(pl.multiple_of(n * GLA_CHUNK, GLA_CHUNK), GLA_CHUNK)})
        for ch in chains:
            g = dirs[ch["d"]][0][0, ch["rows"], :]
            g_hi = g.astype(BF16)
            g_lo = (g - g_hi.astype(F32)).astype(BF16)
            ch["b2"] = _dot(tris[ch["d"]], jnp.concatenate([g_hi, g_lo], axis=1))
        for ch in chains:
            last = dirs[ch["d"]][2]
            b = ch["b2"][:, :LANES] + ch["b2"][:, LANES:]
            b_last = b[last:last + 1, :]
            q_e = gq_ref[0, ch["rows"], :] * jnp.exp(b)
            k = gk_ref[0, ch["rows"], :]
            ch["qm2"] = jnp.concatenate([q_e * masks[0], q_e * masks[1]], axis=0).astype(BF16)
            ch["k_e"] = (k * jnp.exp(-b)).astype(BF16)
            ch["k_d"] = (k * jnp.exp(b_last - b)).astype(BF16)
            ch["decay"] = jnp.sum(jnp.where(eye, jnp.exp(b_last), 0.0), axis=1, keepdims=True)
        for ch in chains:
            ch["a2"] = _dot_nt(ch["qm2"], ch["k_e"])
            ch["kv"] = _dot_tn(ch["k_d"], gv_ref[0, ch["rows"], :])
        for d in range(2):
            st_ref = dirs[d][3]
            st = st_ref[...]
            for ch in chains:
                if ch["d"] == d:
                    ch["st_prev"] = st.astype(BF16)
                    st = ch["decay"] * st + ch["kv"]
            st_ref[...] = st
        for ch in chains:
            d = ch["d"]
            a2 = jnp.where(keep2[d], ch["a2"], 0.0).astype(BF16)
            o2 = _dot(jnp.concatenate([ch["qm2"], a2], axis=1),
                      jnp.concatenate([ch["st_prev"], gv_ref[0, ch["rows"], :]], axis=0))
            for hh in range(2):
                rows_h = slice(GLA_CHUNK * hh, GLA_CHUNK * (hh + 1))
                cols = slice(GLA_DV * hh, GLA_DV * (hh + 1))
                dirs[d][4][ch["rows"], cols] = o2[rows_h, cols]
        return carry

    lax.fori_loop(0, n_chunks // GLA_GROUP, step, 0)

    def finish(i, carry):
        rows = pl.ds(pl.multiple_of(i * GLA_FINAL_ROWS, GLA_FINAL_ROWS), GLA_FINAL_ROWS)
        for hh in range(2):
            cols = slice(GLA_DV * hh, GLA_DV * (hh + 1))
            o = _rms(ofwd_ref[rows, cols] + obwd_ref[rows, cols], gain_ref[...])
            o_ref[0, rows, cols] = (o * gr_ref[0, rows, cols]).astype(BF16)
        return carry

    lax.fori_loop(0, seq // GLA_FINAL_ROWS, finish, 0)


def _gla(gq, gk, gv, gf, gb, gr, s0, gain):
    b, n, _ = gq.shape
    assert (n // GLA_CHUNK) % GLA_GROUP == 0 and n % GLA_FINAL_ROWS == 0
    n_pairs = GLA_HEADS // 2
    key_spec = pl.BlockSpec((1, n, LANES), lambda bi, pi: (bi, 0, pi))
    val_spec = pl.BlockSpec((1, n, 2 * GLA_DV), lambda bi, pi: (bi, 0, pi))
    return pl.pallas_call(
        _gla_kernel,
        grid=(b, n_pairs),
        in_specs=[key_spec, key_spec, val_spec, key_spec, key_spec, val_spec,
                  pl.BlockSpec((1, LANES, 2 * GLA_DV), lambda bi, pi: (pi, 0, 0)),
                  pl.BlockSpec((1, GLA_DV), lambda bi, pi: (0, 0))],
        out_specs=val_spec,
        out_shape=jax.ShapeDtypeStruct((b, n, GLA_VAL_WIDTH), BF16),
        scratch_shapes=[pltpu.VMEM((n, 2 * GLA_DV), F32), pltpu.VMEM((n, 2 * GLA_DV), F32),
                        pltpu.VMEM((LANES, 2 * GLA_DV), F32), pltpu.VMEM((LANES, 2 * GLA_DV), F32)],
        compiler_params=pltpu.CompilerParams(dimension_semantics=("arbitrary", "arbitrary"),
                                             vmem_limit_bytes=VMEM_LIMIT),
        name="gla",
    )(gq, gk, gv, gf, gb, gr, s0, gain)


def _out_ffn_kernel(x_ref, na_ref, gla_ref, wout_ref, gffn_ref, wg_ref, wu_ref, wd_ref, gfin_ref, o_ref):
    subs = [slice(r, r + FFN_SUB_ROWS) for r in range(0, x_ref.shape[0], FFN_SUB_ROWS)]
    mixes = [_dot(na_ref[rows, :], wout_ref[0:NA_WIDTH, :]) + _dot(gla_ref[rows, :], wout_ref[NA_WIDTH:, :])
             for rows in subs]
    hs = [x_ref[rows, :] + mix for rows, mix in zip(subs, mixes)]
    hns = [_rms(h, gffn_ref[...]).astype(BF16) for h in hs]
    acts = [(_silu(_dot(hn, wg_ref[...])) * _dot(hn, wu_ref[...])).astype(BF16) for hn in hns]
    outs = [h + _dot(act, wd_ref[...]) for h, act in zip(hs, acts)]
    for rows, out in zip(subs, outs):
        o_ref[rows, :] = _rms(out, gfin_ref[...])


def _out_ffn(x2d, na2d, gla2d, w_out, g_ffn, w_gate, w_up, w_down, g_fin, tm):
    m, d = x2d.shape
    const = lambda shape: pl.BlockSpec(shape, lambda i: (0, 0))
    row = lambda width: pl.BlockSpec((tm, width), lambda i: (i, 0))
    return pl.pallas_call(
        _out_ffn_kernel,
        grid=(m // tm,),
        in_specs=[row(d), row(NA_WIDTH), row(GLA_VAL_WIDTH), const(w_out.shape), const((1, d)),
                  const(w_gate.shape), const(w_up.shape), const(w_down.shape), const((1, d))],
        out_specs=row(d),
        out_shape=jax.ShapeDtypeStruct((m, d), F32),
        compiler_params=pltpu.CompilerParams(dimension_semantics=("arbitrary",),
                                             vmem_limit_bytes=VMEM_LIMIT_FFN),
        name="out_ffn",
    )(x2d, na2d, gla2d, w_out, g_ffn, w_gate, w_up, w_down, g_fin)


def _na_bias_pairs(rpb):
    cols = np.arange(GRID_W)
    col_start = np.clip(cols - NA_KW // 2, 0, GRID_W - NA_KW)
    valid = (cols[None, :] >= col_start[:, None]) & (cols[None, :] < col_start[:, None] + NA_KW)
    dc = np.clip(cols[None, :] - cols[:, None], -(NA_KW - 1), NA_KW - 1) + (NA_KW - 1)
    onehot = (dc[None] == np.arange(2 * NA_KW - 1)[:, None, None]).astype(np.float32)
    expanded = jnp.einsum("hro,oqk->hrqk", rpb, onehot, precision=lax.Precision.HIGHEST)
    tiles = jnp.where(valid[None, None], expanded * LOG2_E, NEG_INF).astype(F32)
    return jnp.concatenate([tiles[:, :-1], tiles[:, 1:]], axis=-1)


def _na_meta_bias_pairs(meta_bias):
    return jnp.repeat(meta_bias.reshape(NA_HEADS // 2, 2, N_META) * LOG2_E, GRID_W, axis=1)


def kernel(x, meta_tokens, norm_mix_gain, w_in, rpb, meta_bias, w_gate_up_fwd, b_gate_fwd, w_gate_up_bwd, b_gate_bwd, gla_norm_gain, w_out, norm_ffn_gain, w_ffn_gate, w_ffn_up, w_ffn_down, norm_final_gain):
    bsz, seq, d = x.shape
    assert w_in.shape[0] == 1, "single-layer block: meta-token outputs are never consumed"
    assert w_in.shape[2] == IN_WIDTH and seq % GRID_W == 0 and GLA_CHUNK == GRID_W

    w_bf = w_in[0].astype(BF16)
    wupf = jnp.concatenate([w_gate_up_fwd[0], jnp.zeros_like(w_gate_up_bwd[0])], axis=0).astype(BF16)
    wupb = jnp.concatenate([jnp.zeros_like(w_gate_up_fwd[0]), w_gate_up_bwd[0]], axis=0).astype(BF16)
    bgf = b_gate_fwd[0][None, :]
    bgb = b_gate_bwd[0][None, :]
    gain_mix = norm_mix_gain[0][None, :]

    x2d = x.reshape(bsz * seq, d)
    proj_args = (gain_mix, w_bf, wupf, bgf, wupb, bgb)
    naq, nak, nav, gq, gk, gv, gr, gf, gb = _in_proj(x2d, *proj_args, tm=1024)
    _, km, vm, _, gk_m, gv_m, _, gf_m, _ = _in_proj(meta_tokens, *proj_args, tm=N_META)

    s0 = _gla_init(gk_m, gv_m, gf_m)

    per_batch = lambda a: a.reshape(bsz, seq, a.shape[-1])
    na_out = _na(per_batch(naq), per_batch(nak), per_batch(nav), km, vm,
                 _na_bias_pairs(rpb[0]), _na_meta_bias_pairs(meta_bias[0]))
    gla_out = _gla(per_batch(gq), per_batch(gk), per_batch(gv), per_batch(gf), per_batch(gb),
                   per_batch(gr), s0, gla_norm_gain[0][None, :])

    out = _out_ffn(x2d, na_out.reshape(bsz * seq, NA_WIDTH), gla_out.reshape(bsz * seq, GLA_VAL_WIDTH),
                   w_out[0].astype(BF16), norm_ffn_gain[0][None, :], w_ffn_gate[0].astype(BF16),
                   w_ffn_up[0].astype(BF16), w_ffn_down[0].astype(BF16), norm_final_gain[None, :], tm=1024)
    return out.reshape(bsz, seq, d)
```
